```python
import math
import jax, jax.numpy as jnp
from jax import lax
import numpy as np

D_MODEL = 1024
BATCH = 32
SEQ = 2048
DEPTH = 4

CHUNK = 64
EPS = 1e-6
NEG_INF = -1e30
N_NORMS = 6
D_FF = 2816
FFN_RES = 0.5
ROPE_THETA = 10000.0
C_CONV = D_MODEL // 2
CONV_K = 31
H_DIFF = 4
HD_DIFF = 64
DV_DIFF = 2 * HD_DIFF
W_DIFF = H_DIFF * 2 * HD_DIFF
Q_BLOCK = 128
H_CH = 8
HD_CH = 64
W_CH = H_CH * HD_CH
PAST_CHUNKS = 8
BAND = (PAST_CHUNKS + 1) * CHUNK
REL_PAST_MAX = 256
REL_FUT_MAX = CHUNK - 1
N_REL = REL_PAST_MAX + REL_FUT_MAX + 1
N_BRANCH = 3
OFF_GLU = 0
OFF_DQ = OFF_GLU + 2 * C_CONV
OFF_DK = OFF_DQ + W_DIFF
OFF_DV = OFF_DK + W_DIFF
OFF_CQ = OFF_DV + W_DIFF
OFF_CK = OFF_CQ + W_CH
OFF_CV = OFF_CK + W_CH
OFF_GATE = OFF_CV + W_CH
D_IN = OFF_GATE + N_BRANCH * D_MODEL

kernel_name = "hybrid_gated_conv_diffattn_chunkattn_macaron"


def rmsnorm(x, g):
    xf = x.astype(jnp.float32)
    y = xf * lax.rsqrt(jnp.mean(xf * xf, axis=-1, keepdims=True) + EPS)
    return (y * g.astype(jnp.float32)).astype(x.dtype)


def layernorm(x, g, b):
    xf = x.astype(jnp.float32)
    mu = jnp.mean(xf, axis=-1, keepdims=True)
    var = jnp.mean(jnp.square(xf - mu), axis=-1, keepdims=True)
    y = (xf - mu) * lax.rsqrt(var + EPS)
    return (y * g.astype(jnp.float32) + b.astype(jnp.float32)).astype(x.dtype)


def swiglu(h, w_in, w_out):
    gate, up = jnp.split(h @ w_in, 2, axis=-1)
    return (jax.nn.silu(gate) * up) @ w_out


def rope(t, cos, sin):
    half = t.shape[-1] // 2
    c = cos[None, :, None, None, :]
    s = sin[None, :, None, None, :]
    t1, t2 = t[..., :half], t[..., half:]
    return jnp.concatenate([t1 * c - t2 * s, t2 * c + t1 * s], axis=-1)


def conv_module(u, dw, db, ln_g, ln_b, w_out):
    a, b = jnp.split(u, 2, axis=-1)
    z = a * jax.nn.sigmoid(b)
    z = lax.conv_general_dilated(z, dw[:, None, :], window_strides=(1,),
                                 padding=[(CONV_K - 1, 0)],
                                 dimension_numbers=('NWC', 'WIO', 'NWC'),
                                 feature_group_count=C_CONV) + db
    z = jax.nn.silu(layernorm(z, ln_g, ln_b))
    return z @ w_out


def diff_attention(q, k, v, lam, lam_init):
    B, S = q.shape[0], q.shape[1]
    nqb = S // Q_BLOCK
    scale = HD_DIFF ** -0.5
    kchunk = jnp.arange(S) // CHUNK
    qb = jnp.moveaxis(q.reshape(B, nqb, Q_BLOCK, H_DIFF, 2, HD_DIFF), 1, 0)

    def block(args):
        qi, idx = args
        s = jnp.einsum('bqhcd,bkhcd->bhcqk', qi, k).astype(jnp.float32) * scale
        qchunk = (idx * Q_BLOCK + jnp.arange(Q_BLOCK)) // CHUNK
        mask = kchunk[None, :] <= qchunk[:, None]
        p = jax.nn.softmax(jnp.where(mask, s, NEG_INF), axis=-1)
        a = p[:, :, 0] - lam * p[:, :, 1]
        return jnp.einsum('bhqk,bkhe->bqhe', a.astype(v.dtype), v)

    o = lax.map(block, (qb, jnp.arange(nqb)))
    o = jnp.moveaxis(o, 0, 1).reshape(B, S, H_DIFF, DV_DIFF)
    of = o.astype(jnp.float32)
    of = of * lax.rsqrt(jnp.mean(of * of, axis=-1, keepdims=True) + EPS)
    return (of * (1.0 - lam_init)).astype(v.dtype).reshape(B, S, W_DIFF)


def chunk_attention(q, k, v, rel_bias):
    B, S = q.shape[0], q.shape[1]
    nc = S // CHUNK
    pad = PAST_CHUNKS * CHUNK
    kp = jnp.pad(k, ((0, 0), (pad, 0), (0, 0), (0, 0)))
    vp = jnp.pad(v, ((0, 0), (pad, 0), (0, 0), (0, 0)))
    i = jnp.arange(CHUNK)[:, None]
    r = jnp.arange(BAND)[None, :]
    rel = jnp.clip(i - r + pad, -REL_FUT_MAX, REL_PAST_MAX) + REL_FUT_MAX
    bias = rel_bias[:, rel].astype(jnp.float32)
    scale = HD_CH ** -0.5

    def one(c):
        qc = lax.dynamic_slice_in_dim(q, c * CHUNK, CHUNK, axis=1)
        kc = lax.dynamic_slice_in_dim(kp, c * CHUNK, BAND, axis=1)
        vc = lax.dynamic_slice_in_dim(vp, c * CHUNK, BAND, axis=1)
        s = jnp.einsum('bqhd,bkhd->bhqk', qc, kc).astype(jnp.float32) * scale + bias
        valid = (c - PAST_CHUNKS) * CHUNK + jnp.arange(BAND) >= 0
        p = jax.nn.softmax(jnp.where(valid, s, NEG_INF), axis=-1)
        return jnp.einsum('bhqk,bkhd->bqhd', p.astype(v.dtype), vc)

    o = lax.map(one, jnp.arange(nc))
    return jnp.moveaxis(o, 0, 1).reshape(B, S, W_CH)


def setup_inputs(seed: int = 0) -> dict:
    key = jax.random.key(seed)
    ks = jax.random.split(key, 24)
    f32 = jnp.float32

    def w(k, shape, fan_in):
        return jax.random.normal(k, shape, f32) * fan_in ** -0.5

    x = jax.random.normal(ks[0], (BATCH, SEQ, D_MODEL), f32)
    norm_g = 1.0 + 0.05 * jax.random.normal(ks[1], (DEPTH, N_NORMS, D_MODEL), f32)
    ffn1_in = w(ks[2], (DEPTH, D_MODEL, 2 * D_FF), D_MODEL)
    ffn1_out = w(ks[3], (DEPTH, D_FF, D_MODEL), D_FF)
    w_in = w(ks[4], (DEPTH, D_MODEL, D_IN), D_MODEL)
    gate_b = 0.1 * jax.random.normal(ks[5], (DEPTH, N_BRANCH * D_MODEL), f32)
    conv_dw = w(ks[6], (DEPTH, CONV_K, C_CONV), CONV_K)
    conv_b = 0.02 * jax.random.normal(ks[7], (DEPTH, C_CONV), f32)
    conv_ln = jnp.stack([1.0 + 0.05 * jax.random.normal(ks[8], (DEPTH, C_CONV), f32),
                         0.02 * jax.random.normal(ks[9], (DEPTH, C_CONV), f32)], axis=1)
    w_conv_out = w(ks[10], (DEPTH, C_CONV, D_MODEL), C_CONV)
    diff_lambda = 0.1 * jax.random.normal(ks[11], (DEPTH, 4, HD_DIFF), f32)
    w_diff_out = w(ks[12], (DEPTH, W_DIFF, D_MODEL), W_DIFF)
    rel_bias = 0.2 * jax.random.normal(ks[13], (DEPTH, H_CH, N_REL), f32)
    w_chunk_out = w(ks[14], (DEPTH, W_CH, D_MODEL), W_CH)
    w_o = w(ks[15], (DEPTH, D_MODEL, D_MODEL), D_MODEL)
    ffn2_in = w(ks[16], (DEPTH, D_MODEL, 2 * D_FF), D_MODEL)
    ffn2_out = w(ks[17], (DEPTH, D_FF, D_MODEL), D_FF)
    return {"x": x, "norm_g": norm_g, "ffn1_in": ffn1_in, "ffn1_out": ffn1_out,
            "w_in": w_in, "gate_b": gate_b, "conv_dw": conv_dw, "conv_b": conv_b,
            "conv_ln": conv_ln, "w_conv_out": w_conv_out, "diff_lambda": diff_lambda,
            "w_diff_out": w_diff_out, "rel_bias": rel_bias, "w_chunk_out": w_chunk_out,
            "w_o": w_o, "ffn2_in": ffn2_in, "ffn2_out": ffn2_out}


def reference(x, norm_g, ffn1_in, ffn1_out, w_in, gate_b, conv_dw, conv_b, conv_ln,
              w_conv_out, diff_lambda, w_diff_out, rel_bias, w_chunk_out, w_o,
              ffn2_in, ffn2_out):
    B, S, D = x.shape
    pos = jnp.arange(S, dtype=jnp.float32)
    inv_freq = ROPE_THETA ** (-jnp.arange(0, HD_DIFF, 2, dtype=jnp.float32) / HD_DIFF)
    ang = pos[:, None] * inv_freq[None, :]
    cos = jnp.cos(ang).astype(x.dtype)
    sin = jnp.sin(ang).astype(x.dtype)

    for l in range(DEPTH):
        g = norm_g[l]
        x = x + FFN_RES * rmsnorm(swiglu(rmsnorm(x, g[0]), ffn1_in[l], ffn1_out[l]), g[1])

        h = rmsnorm(x, g[2])
        p = h @ w_in[l]
        y_a = conv_module(p[..., OFF_GLU:OFF_DQ], conv_dw[l], conv_b[l],
                          conv_ln[l, 0], conv_ln[l, 1], w_conv_out[l])
        dq = rope(p[..., OFF_DQ:OFF_DK].reshape(B, S, H_DIFF, 2, HD_DIFF), cos, sin)
        dk = rope(p[..., OFF_DK:OFF_DV].reshape(B, S, H_DIFF, 2, HD_DIFF), cos, sin)
        dv = p[..., OFF_DV:OFF_CQ].reshape(B, S, H_DIFF, DV_DIFF)
        lam_init = 0.8 - 0.6 * math.exp(-0.3 * l)
        lamp = diff_lambda[l].astype(jnp.float32)
        lam = (jnp.exp(jnp.sum(lamp[0] * lamp[1])) - jnp.exp(jnp.sum(lamp[2] * lamp[3]))
               + lam_init)
        y_b = diff_attention(dq, dk, dv, lam, lam_init) @ w_diff_out[l]
        cq = p[..., OFF_CQ:OFF_CK].reshape(B, S, H_CH, HD_CH)
        ck = p[..., OFF_CK:OFF_CV].reshape(B, S, H_CH, HD_CH)
        cv = p[..., OFF_CV:OFF_GATE].reshape(B, S, H_CH, HD_CH)
        y_c = chunk_attention(cq, ck, cv, rel_bias[l]) @ w_chunk_out[l]
        gates = jax.nn.sigmoid(p[..., OFF_GATE:] + gate_b[l]).reshape(B, S, N_BRANCH, D)
        mix = gates[:, :, 0] * y_a + gates[:, :, 1] * y_b + gates[:, :, 2] * y_c
        x = x + rmsnorm(mix @ w_o[l], g[3])

        x = x + FFN_RES * rmsnorm(swiglu(rmsnorm(x, g[4]), ffn2_in[l], ffn2_out[l]), g[5])
    return x
```

```python
import functools
import math

import jax
import jax.numpy as jnp
from jax import lax
from jax.experimental import pallas as pl
from jax.experimental.pallas import tpu as pltpu

F32 = jnp.float32
BF16 = jnp.bfloat16

D_MODEL = 1024
CHUNK = 64
EPS = 1e-6
NEG_INF = -1e30
N_NORMS = 6
D_FF = 2816
FFN_RES = 0.5
ROPE_THETA = 10000.0
C_CONV = D_MODEL // 2
CONV_K = 31
H_DIFF = 4
HD_DIFF = 64
DV_DIFF = 2 * HD_DIFF
W_DIFF = H_DIFF * 2 * HD_DIFF
H_CH = 8
HD_CH = 64
W_CH = H_CH * HD_CH
PAST_CHUNKS = 8
REL_PAST_MAX = 256
REL_FUT_MAX = CHUNK - 1
N_REL = REL_PAST_MAX + REL_FUT_MAX + 1
N_BRANCH = 3
OFF_GLU = 0
OFF_DQ = OFF_GLU + 2 * C_CONV
OFF_DK = OFF_DQ + W_DIFF
OFF_DV = OFF_DK + W_DIFF
OFF_CQ = OFF_DV + W_DIFF
OFF_CK = OFF_CQ + W_CH
OFF_CV = OFF_CK + W_CH
OFF_GATE = OFF_CV + W_CH
D_IN = OFF_GATE + N_BRANCH * D_MODEL

LANES = 128
SUBLANES = 8
MXU_DIM = 256
VMEM_BYTES = 64 * 1024 * 1024
MIB = 1024 * 1024

FFN_TM = 1024
FFN_TF = MXU_DIM
INPROJ_TM = 512
CONV_TS = 256
CONV_HALO = 32
CONV_SUB = 32
DIFF_TQ = 256
CH_TQ = 256
CH_WIN = CH_TQ + PAST_CHUNKS * CHUNK
MIX_TM = 512


def _vmem_limit(nbytes):
    return int(min(nbytes + 16 * MIB, VMEM_BYTES - 8 * MIB))


def _rms(x, g):
    return x * lax.rsqrt(jnp.mean(x * x, axis=-1, keepdims=True) + EPS) * g


def _sigmoid(x):
    return 1.0 / (1.0 + jnp.exp(-x))


def _dot(a, b):
    return jnp.dot(a, b, preferred_element_type=F32)


def _dot_nt(a, b):
    return lax.dot_general(a, b, (((1,), (1,)), ((), ())), preferred_element_type=F32)


def _ffn_kernel(x_ref, g_ref, win_ref, wout_ref, o_ref, h_scr, acc_scr, *, g_pre, g_post, nk):
    k = pl.program_id(1)

    @pl.when(k == 0)
    def _():
        h_scr[...] = _rms(x_ref[...], g_ref[g_pre:g_pre + 1, :]).astype(BF16)

    gu = _dot(h_scr[...], win_ref[...])
    gate = gu[:, :FFN_TF]
    up = gu[:, FFN_TF:]
    act = (gate * _sigmoid(gate) * up).astype(BF16)
    contrib = _dot(act, wout_ref[...])

    @pl.when(k == 0)
    def _():
        acc_scr[...] = contrib

    @pl.when(k > 0)
    def _():
        acc_scr[...] += contrib

    @pl.when(k == nk - 1)
    def _():
        o_ref[...] = x_ref[...] + FFN_RES * _rms(acc_scr[...], g_ref[g_post:g_post + 1, :])


def _ffn(x, norm_g, w_in_r, w_out, layer, g_pre, g_post):
    t, d = x.shape
    nk = D_FF // FFN_TF
    tm = min(FFN_TM, t)
    est = (2 * 2 * tm * d * 4 + tm * d * 2 + tm * d * 4
           + 2 * (d * 2 * FFN_TF * 2 + FFN_TF * d * 2) + 3 * tm * 2 * FFN_TF * 4)
    return pl.pallas_call(
        functools.partial(_ffn_kernel, g_pre=g_pre, g_post=g_post, nk=nk),
        grid=(t // tm, nk),
        in_specs=[
            pl.BlockSpec((tm, d), lambda i, k: (i, 0)),
            pl.BlockSpec((None, N_NORMS, d), lambda i, k: (layer, 0, 0)),
            pl.BlockSpec((None, None, d, 2 * FFN_TF), lambda i, k: (layer, k, 0, 0)),
            pl.BlockSpec((None, FFN_TF, d), lambda i, k: (layer, k, 0)),
        ],
        out_specs=pl.BlockSpec((tm, d), lambda i, k: (i, 0)),
        out_shape=jax.ShapeDtypeStruct((t, d), F32),
        scratch_shapes=[pltpu.VMEM((tm, d), BF16), pltpu.VMEM((tm, d), F32)],
        compiler_params=pltpu.CompilerParams(
            dimension_semantics=("parallel", "arbitrary"), vmem_limit_bytes=_vmem_limit(est)),
        name="ffn",
    )(x, norm_g, w_in_r, w_out)


def _inproj_kernel(x_ref, g_ref, w_ref, gb_ref, cos_ref, sin_ref,
                   z_ref, dq_ref, dk_ref, dv_ref, cq_ref, ck_ref, cv_ref, gate_ref):
    h = _rms(x_ref[...], g_ref[2:3, :]).astype(BF16)

    def proj(off, width):
        return _dot(h, w_ref[:, off:off + width])

    u = proj(OFF_GLU, 2 * C_CONV)
    z_ref[...] = u[:, :C_CONV] * _sigmoid(u[:, C_CONV:])

    cos = cos_ref[...]
    sin = sin_ref[...]
    lane = lax.broadcasted_iota(jnp.int32, (1, W_DIFF), 1)
    first_half = (lane % HD_DIFF) < (HD_DIFF // 2)

    def rope(t):
        partner = jnp.where(first_half,
                            pltpu.roll(t, W_DIFF - HD_DIFF // 2, 1),
                            pltpu.roll(t, HD_DIFF // 2, 1))
        return t * cos + partner * sin

    dq_ref[...] = (rope(proj(OFF_DQ, W_DIFF)) * (HD_DIFF ** -0.5)).astype(BF16)
    dk_ref[...] = rope(proj(OFF_DK, W_DIFF)).astype(BF16)
    dv_ref[...] = proj(OFF_DV, W_DIFF).astype(BF16)
    cq_ref[...] = (proj(OFF_CQ, W_CH) * (HD_CH ** -0.5)).astype(BF16)
    ck_ref[...] = proj(OFF_CK, W_CH).astype(BF16)
    cv_ref[...] = proj(OFF_CV, W_CH).astype(BF16)
    for j in range(N_BRANCH):
        lo = j * D_MODEL
        pg = proj(OFF_GATE + lo, D_MODEL) + gb_ref[:, lo:lo + D_MODEL]
        gate_ref[:, lo:lo + D_MODEL] = _sigmoid(pg).astype(BF16)


def _inproj(x, norm_g, w_in, gate_b, cos_t, sin_t, layer, seq):
    t, d = x.shape
    tm = min(INPROJ_TM, seq)
    ns = seq // tm
    est = (d * D_IN * 2 + 2 * tm * d * 4 + 2 * tm * (C_CONV * 4 + 6 * W_DIFF * 2 + 3 * d * 2)
           + 2 * 2 * tm * W_DIFF * 4 + 4 * tm * d * 4)
    row = lambda i: (i, 0)
    half = lambda dt: jax.ShapeDtypeStruct((t, W_DIFF), dt)
    return pl.pallas_call(
        _inproj_kernel,
        grid=(t // tm,),
        in_specs=[
            pl.BlockSpec((tm, d), row),
            pl.BlockSpec((None, N_NORMS, d), lambda i: (layer, 0, 0)),
            pl.BlockSpec((None, d, D_IN), lambda i: (layer, 0, 0), pipeline_mode=pl.Buffered(1)),
            pl.BlockSpec((None, 1, N_BRANCH * d), lambda i: (layer, 0, 0)),
            pl.BlockSpec((tm, W_DIFF), lambda i: (i % ns, 0)),
            pl.BlockSpec((tm, W_DIFF), lambda i: (i % ns, 0)),
        ],
        out_specs=[pl.BlockSpec((tm, C_CONV), row)] + [pl.BlockSpec((tm, W_DIFF), row)] * 6
        + [pl.BlockSpec((tm, N_BRANCH * d), row)],
        out_shape=[jax.ShapeDtypeStruct((t, C_CONV), F32)] + [half(BF16)] * 6
        + [jax.ShapeDtypeStruct((t, N_BRANCH * d), BF16)],
        compiler_params=pltpu.CompilerParams(
            dimension_semantics=("parallel",), vmem_limit_bytes=_vmem_limit(est)),
        name="inproj",
    )(x, norm_g, w_in, gate_b, cos_t, sin_t)


def _conv_kernel(z_ref, halo_ref, dw_ref, db_ref, ln_ref, o_ref, win_scr, y_scr, *, ts):
    j = pl.program_id(1)
    nlt = C_CONV // LANES
    halo = jnp.where(j == 0, 0.0, halo_ref[...])
    for c in range(nlt):
        win_scr[c, 0:CONV_HALO, :] = halo[:, c * LANES:(c + 1) * LANES]
        win_scr[c, CONV_HALO:, :] = z_ref[:, c * LANES:(c + 1) * LANES]
    lead = CONV_HALO - (CONV_K - 1)
    for sub in range(ts // CONV_SUB):
        r0 = sub * CONV_SUB
        for c in range(nlt):
            acc = jnp.zeros((CONV_SUB, LANES), F32)
            for k in range(CONV_K):
                tap = dw_ref[k:k + 1, c * LANES:(c + 1) * LANES]
                acc = acc + tap * win_scr[c, r0 + lead + k:r0 + lead + k + CONV_SUB, :]
            y_scr[r0:r0 + CONV_SUB, c * LANES:(c + 1) * LANES] = acc
    y = y_scr[...] + db_ref[...]
    mu = jnp.mean(y, axis=-1, keepdims=True)
    yc = y - mu
    var = jnp.mean(yc * yc, axis=-1, keepdims=True)
    yn = yc * lax.rsqrt(var + EPS) * ln_ref[0:1, :] + ln_ref[1:2, :]
    o_ref[...] = (yn * _sigmoid(yn)).astype(BF16)


def _conv(z, conv_dw, conv_b, conv_ln, layer, batch, seq):
    t = z.shape[0]
    ts = min(CONV_TS, seq)
    ns = seq // ts
    hb = ts // CONV_HALO
    est = 2 * (ts + CONV_HALO) * C_CONV * 4 + (ts + CONV_HALO) * C_CONV * 4 + 4 * ts * C_CONV * 4
    return pl.pallas_call(
        functools.partial(_conv_kernel, ts=ts),
        grid=(batch, ns),
        in_specs=[
            pl.BlockSpec((ts, C_CONV), lambda b, j: (b * ns + j, 0)),
            pl.BlockSpec((CONV_HALO, C_CONV), lambda b, j: (jnp.maximum((b * ns + j) * hb - 1, 0), 0)),
            pl.BlockSpec((None, CONV_K, C_CONV), lambda b, j: (layer, 0, 0)),
            pl.BlockSpec((None, 1, C_CONV), lambda b, j: (layer, 0, 0)),
            pl.BlockSpec((None, 2, C_CONV), lambda b, j: (layer, 0, 0)),
        ],
        out_specs=pl.BlockSpec((ts, C_CONV), lambda b, j: (b * ns + j, 0)),
        out_shape=jax.ShapeDtypeStruct((t, C_CONV), BF16),
        scratch_shapes=[pltpu.VMEM((C_CONV // LANES, ts + CONV_HALO, LANES), F32),
                        pltpu.VMEM((ts, C_CONV), F32)],
        compiler_params=pltpu.CompilerParams(
            dimension_semantics=("parallel", "parallel"), vmem_limit_bytes=_vmem_limit(est)),
        name="conv",
    )(z, z, conv_dw, conv_b, conv_ln)


def _diff_kernel(lam_ref, q_ref, k_ref, v_ref, o_ref, m_scr, l_scr, acc_scr, *, tq, lam_init):
    i = pl.program_id(2)
    lamp = lam_ref[...]
    lam = (jnp.exp(jnp.sum(lamp[0:1, :] * lamp[1:2, :], keepdims=True))
           - jnp.exp(jnp.sum(lamp[2:3, :] * lamp[3:4, :], keepdims=True)) + lam_init)

    q = q_ref[...]
    lane = lax.broadcasted_iota(jnp.int32, (1, DV_DIFF), 1)
    comp0 = lane < HD_DIFF
    zero = jnp.zeros_like(q)
    qq = jnp.concatenate([jnp.where(comp0, q, zero), jnp.where(comp0, zero, q)], axis=0)

    m_scr[...] = jnp.full(m_scr.shape, -jnp.inf, F32)
    l_scr[...] = jnp.zeros(l_scr.shape, F32)
    acc_scr[...] = jnp.zeros(acc_scr.shape, F32)

    def step(kb, masked):
        start = pl.multiple_of(kb * tq, tq)
        kblk = k_ref[pl.ds(start, tq), :]
        vblk = v_ref[pl.ds(start, tq), :]
        s = _dot_nt(qq, kblk)
        if masked:
            qc = lax.broadcasted_iota(jnp.int32, (2 * tq, tq), 0) % tq // CHUNK
            kc = lax.broadcasted_iota(jnp.int32, (2 * tq, tq), 1) // CHUNK
            s = jnp.where(kc <= qc, s, NEG_INF)
        m_old = m_scr[...]
        m_new = jnp.maximum(m_old, jnp.max(s, axis=-1, keepdims=True))
        p = jnp.exp(s - m_new)
        alpha = jnp.exp(m_old - m_new)
        l_scr[...] = alpha * l_scr[...] + jnp.sum(p, axis=-1, keepdims=True)
        acc_scr[...] = alpha * acc_scr[...] + _dot(p.astype(BF16), vblk)
        m_scr[...] = m_new

    def body(kb, carry):
        step(kb, False)
        return carry

    lax.fori_loop(0, i, body, 0)
    step(i, True)

    o_all = acc_scr[...] / l_scr[...]
    o = o_all[:tq] - lam * o_all[tq:]
    o = o * lax.rsqrt(jnp.mean(o * o, axis=-1, keepdims=True) + EPS) * (1.0 - lam_init)
    o_ref[...] = o.astype(BF16)


def _diff_attn(dq, dk, dv, diff_lambda, layer, batch, seq):
    t = dq.shape[0]
    tq = min(DIFF_TQ, seq)
    nq = seq // tq
    lam_init = 0.8 - 0.6 * math.exp(-0.3 * layer)
    est = 2 * 2 * seq * DV_DIFF * 2 + 4 * tq * DV_DIFF * 2 + 2 * tq * (DV_DIFF + 2 * LANES) * 4 + 8 * 2 * tq * tq * 4
    return pl.pallas_call(
        functools.partial(_diff_kernel, tq=tq, lam_init=lam_init),
        grid=(batch, H_DIFF, nq),
        in_specs=[
            pl.BlockSpec((None, 4, HD_DIFF), lambda b, h, i: (layer, 0, 0)),
            pl.BlockSpec((tq, DV_DIFF), lambda b, h, i: (b * nq + i, h)),
            pl.BlockSpec((seq, DV_DIFF), lambda b, h, i: (b, h)),
            pl.BlockSpec((seq, DV_DIFF), lambda b, h, i: (b, h)),
        ],
        out_specs=pl.BlockSpec((tq, DV_DIFF), lambda b, h, i: (b * nq + i, h)),
        out_shape=jax.ShapeDtypeStruct((t, W_DIFF), BF16),
        scratch_shapes=[pltpu.VMEM((2 * tq, 1), F32), pltpu.VMEM((2 * tq, 1), F32),
                        pltpu.VMEM((2 * tq, DV_DIFF), F32)],
        compiler_params=pltpu.CompilerParams(
            dimension_semantics=("parallel", "parallel", "arbitrary"), vmem_limit_bytes=_vmem_limit(est)),
        name="diff_attn",
    )(diff_lambda, dq, dk, dv)


def _chunk_kernel(bm_ref, q_ref, k_ref, v_ref, o_ref, *, tq, win):
    i = pl.program_id(2)
    past = win - tq
    var = jnp.minimum(i, past // tq)
    ws = pl.multiple_of(jnp.maximum(i * tq - past, 0), tq)
    q = q_ref[...]
    kw = k_ref[pl.ds(ws, win), :]
    vw = v_ref[pl.ds(ws, win), :]
    lane = lax.broadcasted_iota(jnp.int32, (1, 2 * HD_CH), 1)
    head0 = lane < HD_CH
    zero = jnp.zeros_like(q)
    outs = []
    for e in range(2):
        qe = jnp.where(head0, q, zero) if e == 0 else jnp.where(head0, zero, q)
        s = _dot_nt(qe, kw) + bm_ref[var, e]
        m = jnp.max(s, axis=-1, keepdims=True)
        p = jnp.exp(s - m)
        l = jnp.sum(p, axis=-1, keepdims=True)
        outs.append(_dot(p.astype(BF16), vw) / l)
    o_ref[...] = jnp.where(head0, outs[0], outs[1]).astype(BF16)


def _chunk_attn(cq, ck, cv, bm, layer, batch, seq):
    t = cq.shape[0]
    tq = min(CH_TQ, seq)
    nq = seq // tq
    win = min(CH_WIN, seq)
    nvar = bm.shape[1]
    est = 2 * nvar * 2 * tq * win * 4 + 2 * 2 * seq * LANES * 2 + 4 * tq * LANES * 2 + 8 * tq * win * 4
    return pl.pallas_call(
        functools.partial(_chunk_kernel, tq=tq, win=win),
        grid=(H_CH // 2, batch, nq),
        in_specs=[
            pl.BlockSpec((None, nvar, 2, tq, win), lambda hp, b, i: (layer, 0, hp, 0, 0)),
            pl.BlockSpec((tq, 2 * HD_CH), lambda hp, b, i: (b * nq + i, hp)),
            pl.BlockSpec((seq, 2 * HD_CH), lambda hp, b, i: (b, hp)),
            pl.BlockSpec((seq, 2 * HD_CH), lambda hp, b, i: (b, hp)),
        ],
        out_specs=pl.BlockSpec((tq, 2 * HD_CH), lambda hp, b, i: (b * nq + i, hp)),
        out_shape=jax.ShapeDtypeStruct((t, W_CH), BF16),
        compiler_params=pltpu.CompilerParams(
            dimension_semantics=("parallel", "parallel", "arbitrary"), vmem_limit_bytes=_vmem_limit(est)),
        name="chunk_attn",
    )(bm, cq, ck, cv)


def _bias_mask(rel_bias, tq, win):
    past = win - tq
    nvar = past // tq + 1
    j = jnp.arange(tq)[:, None]
    r = jnp.arange(win)[None, :]
    mats = []
    for v in range(nvar):
        off = v * tq
        d = j - r + off
        rel = jnp.clip(d, -REL_FUT_MAX, REL_PAST_MAX) + REL_FUT_MAX
        dc = off // CHUNK + j // CHUNK - r // CHUNK
        valid = (dc >= 0) & (dc <= PAST_CHUNKS)
        b = rel_bias[:, :, rel].astype(F32)
        mats.append(jnp.where(valid[None, None], b, NEG_INF))
    return jnp.stack(mats, axis=1)


def _mix_kernel(x_ref, g_ref, za_ref, ob_ref, oc_ref, gate_ref, wa_ref, wb_ref, wc_ref, wo_ref, o_ref):
    d = D_MODEL
    mix = gate_ref[:, 0:d].astype(F32) * _dot(za_ref[...], wa_ref[...])
    mix = mix + gate_ref[:, d:2 * d].astype(F32) * _dot(ob_ref[...], wb_ref[...])
    mix = mix + gate_ref[:, 2 * d:3 * d].astype(F32) * _dot(oc_ref[...], wc_ref[...])
    y = _dot(mix.astype(BF16), wo_ref[...])
    o_ref[...] = x_ref[...] + _rms(y, g_ref[3:4, :])


def _mix(x, norm_g, za, ob, oc, gates, wa, wb, wc, wo, layer):
    t, d = x.shape
    tm = min(MIX_TM, t)
    est = (2 * 2 * tm * d * 4 + 2 * tm * (3 * C_CONV * 2 + 3 * d * 2)
           + 2 * (3 * C_CONV * d * 2 + d * d * 2) + 4 * tm * d * 4)
    row = lambda i: (i, 0)
    wspec = lambda k: pl.BlockSpec((None, k, d), lambda i: (layer, 0, 0))
    return pl.pallas_call(
        _mix_kernel,
        grid=(t // tm,),
        in_specs=[
            pl.BlockSpec((tm, d), row),
            pl.BlockSpec((None, N_NORMS, d), lambda i: (layer, 0, 0)),
            pl.BlockSpec((tm, C_CONV), row), pl.BlockSpec((tm, W_DIFF), row), pl.BlockSpec((tm, W_CH), row),
            pl.BlockSpec((tm, N_BRANCH * d), row),
            wspec(C_CONV), wspec(W_DIFF), wspec(W_CH), wspec(d),
        ],
        out_specs=pl.BlockSpec((tm, d), row),
        out_shape=jax.ShapeDtypeStruct((t, d), F32),
        compiler_params=pltpu.CompilerParams(
            dimension_semantics=("parallel",), vmem_limit_bytes=_vmem_limit(est)),
        name="mix",
    )(x, norm_g, za, ob, oc, gates, wa, wb, wc, wo)


def _ffn_in_layout(w):
    nl, d, _ = w.shape
    nk = D_FF // FFN_TF
    w = w.astype(BF16).reshape(nl, d, 2, nk, FFN_TF)
    return jnp.transpose(w, (0, 3, 1, 2, 4)).reshape(nl, nk, d, 2 * FFN_TF)


def _rope_tables(seq):
    pos = jnp.arange(seq, dtype=F32)
    inv_freq = ROPE_THETA ** (-jnp.arange(0, HD_DIFF, 2, dtype=F32) / HD_DIFF)
    ang = pos[:, None] * inv_freq[None, :]
    reps = W_DIFF // (HD_DIFF // 2)
    cos_t = jnp.tile(jnp.cos(ang), (1, reps))
    sign = jnp.tile(jnp.concatenate([-jnp.ones(HD_DIFF // 2, F32), jnp.ones(HD_DIFF // 2, F32)]),
                    W_DIFF // HD_DIFF)
    sin_t = jnp.tile(jnp.sin(ang), (1, reps)) * sign[None, :]
    return cos_t, sin_t


def kernel(x, norm_g, ffn1_in, ffn1_out, w_in, gate_b, conv_dw, conv_b, conv_ln, w_conv_out,
           diff_lambda, w_diff_out, rel_bias, w_chunk_out, w_o, ffn2_in, ffn2_out):
    batch, seq, d = x.shape
    depth = norm_g.shape[0]
    assert d == D_MODEL and seq % CH_TQ == 0 and seq % DIFF_TQ == 0 and seq % INPROJ_TM == 0

    ffn1_in_r, ffn2_in_r = _ffn_in_layout(ffn1_in), _ffn_in_layout(ffn2_in)
    ffn1_out_b, ffn2_out_b = ffn1_out.astype(BF16), ffn2_out.astype(BF16)
    w_in_b = w_in.astype(BF16)
    wa_b, wb_b, wc_b, wo_b = (w.astype(BF16) for w in (w_conv_out, w_diff_out, w_chunk_out, w_o))
    gate_b3 = gate_b.reshape(depth, 1, N_BRANCH * d)
    conv_b3 = conv_b.reshape(depth, 1, C_CONV)
    cos_t, sin_t = _rope_tables(seq)
    bm = _bias_mask(rel_bias, min(CH_TQ, seq), min(CH_WIN, seq))

    xt = x.reshape(batch * seq, d)
    for l in range(depth):
        xt = _ffn(xt, norm_g, ffn1_in_r, ffn1_out_b, l, 0, 1)
        z, dq, dk, dv, cq, ck, cv, gates = _inproj(xt, norm_g, w_in_b, gate_b3, cos_t, sin_t, l, seq)
        za = _conv(z, conv_dw, conv_b3, conv_ln, l, batch, seq)
        ob = _diff_attn(dq, dk, dv, diff_lambda, l, batch, seq)
        oc = _chunk_attn(cq, ck, cv, bm, l, batch, seq)
        xt = _mix(xt, norm_g, za, ob, oc, gates, wa_b, wb_b, wc_b, wo_b, l)
        xt = _ffn(xt, norm_g, ffn2_in_r, ffn2_out_b, l, 4, 5)
    return xt.reshape(batch, seq, d)
```

```python
import functools
import math

import jax
import jax.numpy as jnp
from jax import lax
from jax.experimental import pallas as pl
from jax.experimental.pallas import tpu as pltpu

F32 = jnp.float32
BF16 = jnp.bfloat16

D_MODEL = 1024
CHUNK = 64
EPS = 1e-6
NEG_INF = -1e30
LOG2_E = math.log2(math.e)
N_NORMS = 6
D_FF = 2816
FFN_RES = 0.5
ROPE_THETA = 10000.0
C_CONV = D_MODEL // 2
CONV_K = 31
H_DIFF = 4
HD_DIFF = 64
DV_DIFF = 2 * HD_DIFF
W_DIFF = H_DIFF * 2 * HD_DIFF
H_CH = 8
HD_CH = 64
W_CH = H_CH * HD_CH
PAST_CHUNKS = 8
REL_PAST_MAX = 256
REL_FUT_MAX = CHUNK - 1
N_REL = REL_PAST_MAX + REL_FUT_MAX + 1
N_BRANCH = 3
OFF_GLU = 0
OFF_DQ = OFF_GLU + 2 * C_CONV
OFF_DK = OFF_DQ + W_DIFF
OFF_DV = OFF_DK + W_DIFF
OFF_CQ = OFF_DV + W_DIFF
OFF_CK = OFF_CQ + W_CH
OFF_CV = OFF_CK + W_CH
OFF_GATE = OFF_CV + W_CH
D_IN = OFF_GATE + N_BRANCH * D_MODEL

LANES = 128
SUBLANES = 8
MXU_DIM = 256
VMEM_BYTES = 64 * 1024 * 1024
MIB = 1024 * 1024

FFN_TM = 512
FFN_TF = MXU_DIM
INPROJ_TM = 512
CONV_TS = 256
CONV_HALO = 32
CONV_SUB = 32
DIFF_TQ = 256
DIFF_VT_ROWS = DV_DIFF + 16
DIFF_LOOKAHEAD = 2
CH_TQ = 256
CH_WIN = CH_TQ + PAST_CHUNKS * CHUNK
CH_LOOKAHEAD = 2
MIX_TM = 512


def _vmem_limit(nbytes):
    return int(min(nbytes + 16 * MIB, VMEM_BYTES - 8 * MIB))


def _rms(x, g):
    return x * lax.rsqrt(jnp.mean(x * x, axis=-1, keepdims=True) + EPS) * g


def _sigmoid(x):
    return 1.0 / (1.0 + jnp.exp(-x))


def _dot(a, b):
    return jnp.dot(a, b, preferred_element_type=F32)


def _dot_nt(a, b):
    return lax.dot_general(a, b, (((1,), (1,)), ((), ())), preferred_element_type=F32)


def _ffn_kernel(x_ref, g_ref, win_ref, wout_ref, o_ref, act_scr, *, g_pre, g_post, nk):
    h = _rms(x_ref[...], g_ref[g_pre:g_pre + 1, :]).astype(BF16)
    for k in range(nk):
        gu = _dot(h, win_ref[k])
        gate = gu[:, :FFN_TF]
        up = gu[:, FFN_TF:]
        act_scr[:, k * FFN_TF:(k + 1) * FFN_TF] = (gate * _sigmoid(gate) * up).astype(BF16)
    y = _dot(act_scr[...], wout_ref[...])
    o_ref[...] = x_ref[...] + FFN_RES * _rms(y, g_ref[g_post:g_post + 1, :])


def _ffn(x, norm_g, w_in_r, w_out, layer, g_pre, g_post):
    t, d = x.shape
    nk = D_FF // FFN_TF
    tm = min(FFN_TM, t)
    est = (2 * 2 * tm * d * 4 + d * 2 * D_FF * 2 + D_FF * d * 2 + tm * D_FF * 2
           + tm * d * 2 + 3 * tm * 2 * FFN_TF * 4 + 2 * tm * d * 4)
    once = pl.Buffered(1)
    return pl.pallas_call(
        functools.partial(_ffn_kernel, g_pre=g_pre, g_post=g_post, nk=nk),
        grid=(t // tm,),
        in_specs=[
            pl.BlockSpec((tm, d), lambda i: (i, 0)),
            pl.BlockSpec((None, N_NORMS, d), lambda i: (layer, 0, 0)),
            pl.BlockSpec((None, nk, d, 2 * FFN_TF), lambda i: (layer, 0, 0, 0), pipeline_mode=once),
            pl.BlockSpec((None, D_FF, d), lambda i: (layer, 0, 0), pipeline_mode=once),
        ],
        out_specs=pl.BlockSpec((tm, d), lambda i: (i, 0)),
        out_shape=jax.ShapeDtypeStruct((t, d), F32),
        scratch_shapes=[pltpu.VMEM((tm, D_FF), BF16)],
        compiler_params=pltpu.CompilerParams(
            dimension_semantics=("parallel",), vmem_limit_bytes=_vmem_limit(est)),
        name="ffn",
    )(x, norm_g, w_in_r, w_out)


def _inproj_kernel(x_ref, g_ref, w_ref, gb_ref, cos_ref, sin_ref,
                   z_ref, dq_ref, dk_ref, dv_ref, cq_ref, ck_ref, cv_ref, gate_ref):
    h = _rms(x_ref[...], g_ref[2:3, :]).astype(BF16)

    def proj(off, width):
        return _dot(h, w_ref[:, off:off + width])

    u = proj(OFF_GLU, 2 * C_CONV)
    z_ref[...] = u[:, :C_CONV] * _sigmoid(u[:, C_CONV:])

    cos = cos_ref[...]
    sin = sin_ref[...]
    lane = lax.broadcasted_iota(jnp.int32, (1, W_DIFF), 1)
    first_half = (lane % HD_DIFF) < (HD_DIFF // 2)

    def rope(t):
        partner = jnp.where(first_half,
                            pltpu.roll(t, W_DIFF - HD_DIFF // 2, 1),
                            pltpu.roll(t, HD_DIFF // 2, 1))
        return t * cos + partner * sin

    dq_ref[...] = (rope(proj(OFF_DQ, W_DIFF)) * (HD_DIFF ** -0.5 * LOG2_E)).astype(BF16)
    dk_ref[...] = rope(proj(OFF_DK, W_DIFF)).astype(BF16)
    dv_ref[...] = proj(OFF_DV, W_DIFF).astype(BF16)
    cq_ref[...] = (proj(OFF_CQ, W_CH) * (HD_CH ** -0.5 * LOG2_E)).astype(BF16)
    ck_ref[...] = proj(OFF_CK, W_CH).astype(BF16)
    cv_ref[...] = proj(OFF_CV, W_CH).astype(BF16)
    for j in range(N_BRANCH):
        lo = j * D_MODEL
        pg = proj(OFF_GATE + lo, D_MODEL) + gb_ref[:, lo:lo + D_MODEL]
        gate_ref[:, lo:lo + D_MODEL] = _sigmoid(pg).astype(BF16)


def _inproj(x, norm_g, w_in, gate_b, cos_t, sin_t, layer, seq):
    t, d = x.shape
    tm = min(INPROJ_TM, seq)
    ns = seq // tm
    est = (d * D_IN * 2 + 2 * tm * d * 4 + 2 * tm * (C_CONV * 4 + 6 * W_DIFF * 2 + 3 * d * 2)
           + 2 * 2 * tm * W_DIFF * 4 + 4 * tm * d * 4)
    row = lambda i: (i, 0)
    half = lambda dt: jax.ShapeDtypeStruct((t, W_DIFF), dt)
    return pl.pallas_call(
        _inproj_kernel,
        grid=(t // tm,),
        in_specs=[
            pl.BlockSpec((tm, d), row),
            pl.BlockSpec((None, N_NORMS, d), lambda i: (layer, 0, 0)),
            pl.BlockSpec((None, d, D_IN), lambda i: (layer, 0, 0), pipeline_mode=pl.Buffered(1)),
            pl.BlockSpec((None, 1, N_BRANCH * d), lambda i: (layer, 0, 0)),
            pl.BlockSpec((tm, W_DIFF), lambda i: (i % ns, 0)),
            pl.BlockSpec((tm, W_DIFF), lambda i: (i % ns, 0)),
        ],
        out_specs=[pl.BlockSpec((tm, C_CONV), row)] + [pl.BlockSpec((tm, W_DIFF), row)] * 6
        + [pl.BlockSpec((tm, N_BRANCH * d), row)],
        out_shape=[jax.ShapeDtypeStruct((t, C_CONV), F32)] + [half(BF16)] * 6
        + [jax.ShapeDtypeStruct((t, N_BRANCH * d), BF16)],
        compiler_params=pltpu.CompilerParams(
            dimension_semantics=("parallel",), vmem_limit_bytes=_vmem_limit(est)),
        name="inproj",
    )(x, norm_g, w_in, gate_b, cos_t, sin_t)


def _conv_kernel(z_ref, halo_ref, dw_ref, db_ref, ln_ref, o_ref, win_scr, y_scr, *, ts):
    j = pl.program_id(1)
    nlt = C_CONV // LANES
    halo = jnp.where(j == 0, 0.0, halo_ref[...])
    for c in range(nlt):
        win_scr[c, 0:CONV_HALO, :] = halo[:, c * LANES:(c + 1) * LANES]
        win_scr[c, CONV_HALO:, :] = z_ref[:, c * LANES:(c + 1) * LANES]
    lead = CONV_HALO - (CONV_K - 1)
    for sub in range(ts // CONV_SUB):
        r0 = sub * CONV_SUB
        for c in range(nlt):
            acc = jnp.zeros((CONV_SUB, LANES), F32)
            for k in range(CONV_K):
                tap = dw_ref[k:k + 1, c * LANES:(c + 1) * LANES]
                acc = acc + tap * win_scr[c, r0 + lead + k:r0 + lead + k + CONV_SUB, :]
            y_scr[r0:r0 + CONV_SUB, c * LANES:(c + 1) * LANES] = acc
    y = y_scr[...] + db_ref[...]
    mu = jnp.mean(y, axis=-1, keepdims=True)
    yc = y - mu
    var = jnp.mean(yc * yc, axis=-1, keepdims=True)
    yn = yc * lax.rsqrt(var + EPS) * ln_ref[0:1, :] + ln_ref[1:2, :]
    o_ref[...] = (yn * _sigmoid(yn)).astype(BF16)


def _conv(z, conv_dw, conv_b, conv_ln, layer, batch, seq):
    t = z.shape[0]
    ts = min(CONV_TS, seq)
    ns = seq // ts
    hb = ts // CONV_HALO
    est = 2 * (ts + CONV_HALO) * C_CONV * 4 + (ts + CONV_HALO) * C_CONV * 4 + 4 * ts * C_CONV * 4
    return pl.pallas_call(
        functools.partial(_conv_kernel, ts=ts),
        grid=(batch, ns),
        in_specs=[
            pl.BlockSpec((ts, C_CONV), lambda b, j: (b * ns + j, 0)),
            pl.BlockSpec((CONV_HALO, C_CONV), lambda b, j: (jnp.maximum((b * ns + j) * hb - 1, 0), 0)),
            pl.BlockSpec((None, CONV_K, C_CONV), lambda b, j: (layer, 0, 0)),
            pl.BlockSpec((None, 1, C_CONV), lambda b, j: (layer, 0, 0)),
            pl.BlockSpec((None, 2, C_CONV), lambda b, j: (layer, 0, 0)),
        ],
        out_specs=pl.BlockSpec((ts, C_CONV), lambda b, j: (b * ns + j, 0)),
        out_shape=jax.ShapeDtypeStruct((t, C_CONV), BF16),
        scratch_shapes=[pltpu.VMEM((C_CONV // LANES, ts + CONV_HALO, LANES), F32),
                        pltpu.VMEM((ts, C_CONV), F32)],
        compiler_params=pltpu.CompilerParams(
            dimension_semantics=("parallel", "parallel"), vmem_limit_bytes=_vmem_limit(est)),
        name="conv",
    )(z, z, conv_dw, conv_b, conv_ln)


def _diff_kernel(lam_ref, q_ref, k_ref, v_ref, o_ref, vt_scr, qq_scr, m_scr, acc_scr,
                 *, tq, nq, lam_init):
    row = lax.broadcasted_iota(jnp.int32, (DIFF_VT_ROWS - DV_DIFF, tq), 0)
    ones_row = jnp.where(row == 0, 1.0, 0.0).astype(BF16)
    for kb in range(nq):
        vt_scr[kb, 0:DV_DIFF, :] = v_ref[kb * tq:(kb + 1) * tq, :].T
        vt_scr[kb, DV_DIFF:, :] = ones_row

    lamp = lam_ref[...]
    lam = (jnp.exp(jnp.sum(lamp[0:1, :] * lamp[1:2, :], keepdims=True))
           - jnp.exp(jnp.sum(lamp[2:3, :] * lamp[3:4, :], keepdims=True)) + lam_init)

    lane = lax.broadcasted_iota(jnp.int32, (1, DV_DIFF), 1)
    comp0 = lane < HD_DIFF
    for qb in range(nq):
        q = q_ref[qb * tq:(qb + 1) * tq, :]
        zero = jnp.zeros_like(q)
        qq_scr[qb, 0:tq, :] = jnp.where(comp0, q, zero)
        qq_scr[qb, tq:2 * tq, :] = jnp.where(comp0, zero, q)

    m_scr[...] = jnp.full(m_scr.shape, -jnp.inf, F32)
    acc_scr[...] = jnp.zeros(acc_scr.shape, F32)

    kc = lax.broadcasted_iota(jnp.int32, (tq, 2 * tq), 0) // CHUNK
    qc = lax.broadcasted_iota(jnp.int32, (tq, 2 * tq), 1) % tq // CHUNK
    visible = kc <= qc

    def scores(qb, kb):
        st = _dot_nt(k_ref[kb * tq:(kb + 1) * tq, :], qq_scr[qb])
        return jnp.where(visible, st, NEG_INF) if kb == qb else st

    def update(qb, kb, st):
        m_old = m_scr[qb]
        m_new = jnp.maximum(m_old, jnp.max(st, axis=0, keepdims=True))
        p = jnp.exp2(st - m_new)
        alpha = jnp.exp2(m_old - m_new)
        acc_scr[qb] = alpha * acc_scr[qb] + _dot(vt_scr[kb], p.astype(BF16))
        m_scr[qb] = m_new

    def finish(qb):
        on = acc_scr[qb, 0:DV_DIFF, :] * (1.0 / acc_scr[qb, DV_DIFF:DV_DIFF + 1, :])
        ot = on[:, :tq] - lam * on[:, tq:]
        ot = ot * lax.rsqrt(jnp.mean(ot * ot, axis=0, keepdims=True) + EPS) * (1.0 - lam_init)
        o_ref[qb * tq:(qb + 1) * tq, :] = ot.T.astype(BF16)

    steps = []
    for a in range(nq // 2):
        b = nq - 1 - a
        seq_a = [(a, a)] + [(a, kb) for kb in range(a)]
        seq_b = [(b, b)] + [(b, kb) for kb in range(b)]
        for n in range(max(len(seq_a), len(seq_b))):
            steps += seq_a[n:n + 1] + seq_b[n:n + 1]
    last = {qb: max(n for n, s in enumerate(steps) if s[0] == qb) for qb in range(nq)}
    pending = [scores(*s) for s in steps[:DIFF_LOOKAHEAD]]
    for n, (qb, kb) in enumerate(steps):
        if n + DIFF_LOOKAHEAD < len(steps):
            pending.append(scores(*steps[n + DIFF_LOOKAHEAD]))
        update(qb, kb, pending.pop(0))
        if last[qb] == n:
            finish(qb)


def _diff_attn(dq, dk, dv, diff_lambda, layer, batch, seq):
    t = dq.shape[0]
    tq = min(DIFF_TQ, seq)
    nq = seq // tq
    assert nq % 2 == 0
    lam_init = 0.8 - 0.6 * math.exp(-0.3 * layer)
    est = (2 * 4 * seq * DV_DIFF * 2 + nq * DIFF_VT_ROWS * tq * 2 + nq * 2 * tq * DV_DIFF * 2
           + nq * (DIFF_VT_ROWS + 8) * 2 * tq * 4 + (DIFF_LOOKAHEAD + 4) * tq * 2 * tq * 4)
    seq_blk = pl.BlockSpec((seq, DV_DIFF), lambda b, h: (b, h))
    return pl.pallas_call(
        functools.partial(_diff_kernel, tq=tq, nq=nq, lam_init=lam_init),
        grid=(batch, H_DIFF),
        in_specs=[pl.BlockSpec((None, 4, HD_DIFF), lambda b, h: (layer, 0, 0)), seq_blk, seq_blk, seq_blk],
        out_specs=seq_blk,
        out_shape=jax.ShapeDtypeStruct((t, W_DIFF), BF16),
        scratch_shapes=[pltpu.VMEM((nq, DIFF_VT_ROWS, tq), BF16),
                        pltpu.VMEM((nq, 2 * tq, DV_DIFF), BF16),
                        pltpu.VMEM((nq, 1, 2 * tq), F32),
                        pltpu.VMEM((nq, DIFF_VT_ROWS, 2 * tq), F32)],
        compiler_params=pltpu.CompilerParams(
            dimension_semantics=("parallel", "parallel"), vmem_limit_bytes=_vmem_limit(est)),
        name="diff_attn",
    )(diff_lambda, dq, dk, dv)


def _chunk_kernel(gv_ref, q_ref, k_ref, v_ref, o_ref, bmt_scr, vt_scr, qq_scr, *, tq, nq, win):
    past = win - tq

    @pl.when(pl.program_id(1) == 0)
    def _():
        rc = lax.broadcasted_iota(jnp.int32, (win, tq), 0) // CHUNK
        jc = lax.broadcasted_iota(jnp.int32, (win, tq), 1) // CHUNK
        in_band = (rc >= jc) & (rc <= jc + PAST_CHUNKS)
        for e in range(2):
            ext = jnp.broadcast_to(gv_ref[e:e + 1, :], (win, 2 * win))
            toep = pltpu.roll(ext, tq, 1, stride=1, stride_axis=0)[:, :tq]
            bmt_scr[:, e * tq:(e + 1) * tq] = jnp.where(in_band, toep * LOG2_E, NEG_INF)

    vt_scr[0:2 * HD_CH, :] = v_ref[...].T
    row = lax.broadcasted_iota(jnp.int32, (vt_scr.shape[0] - 2 * HD_CH, vt_scr.shape[1]), 0)
    vt_scr[2 * HD_CH:, :] = jnp.where(row == 0, 1.0, 0.0).astype(BF16)

    lane = lax.broadcasted_iota(jnp.int32, (1, 2 * HD_CH), 1)
    head0 = lane < HD_CH
    for qb in range(nq):
        q = q_ref[qb * tq:(qb + 1) * tq, :]
        zero = jnp.zeros_like(q)
        qq_scr[qb, 0:tq, :] = jnp.where(head0, q, zero)
        qq_scr[qb, tq:2 * tq, :] = jnp.where(head0, zero, q)

    def window(qb):
        ws = max(qb * tq - past, 0)
        return ws, (qb + 1) * tq - ws

    def scores(qb):
        ws, wl = window(qb)
        return _dot_nt(k_ref[ws:ws + wl, :], qq_scr[qb]) + bmt_scr[win - wl:win, :]

    def finish(qb, st):
        ws, wl = window(qb)
        p = jnp.exp2(st - jnp.max(st, axis=0, keepdims=True))
        acc = _dot(vt_scr[:, ws:ws + wl], p.astype(BF16))
        den = acc[2 * HD_CH:2 * HD_CH + 1, :]
        o0 = acc[0:HD_CH, 0:tq] * (1.0 / den[:, 0:tq])
        o1 = acc[HD_CH:2 * HD_CH, tq:2 * tq] * (1.0 / den[:, tq:2 * tq])
        o_ref[qb * tq:(qb + 1) * tq, :] = jnp.concatenate([o0, o1], axis=0).T.astype(BF16)

    pending = [scores(qb) for qb in range(min(CH_LOOKAHEAD, nq))]
    for qb in range(nq):
        if qb + CH_LOOKAHEAD < nq:
            pending.append(scores(qb + CH_LOOKAHEAD))
        finish(qb, pending.pop(0))


def _chunk_attn(cq, ck, cv, gv, layer, batch, seq):
    t = cq.shape[0]
    tq = min(CH_TQ, seq)
    nq = seq // tq
    win = CH_WIN
    assert seq >= win
    vt_rows = 2 * HD_CH + 16
    est = (win * 2 * tq * 4 + vt_rows * seq * 2 + nq * 2 * tq * 2 * HD_CH * 2 + 2 * 4 * seq * 2 * HD_CH * 2
           + win * 2 * win * 4 * 3 + (CH_LOOKAHEAD + 4) * win * 2 * tq * 4)
    seq_blk = pl.BlockSpec((seq, 2 * HD_CH), lambda hp, b: (b, hp))
    return pl.pallas_call(
        functools.partial(_chunk_kernel, tq=tq, nq=nq, win=win),
        grid=(H_CH // 2, batch),
        in_specs=[pl.BlockSpec((None, None, 2, 2 * win), lambda hp, b: (layer, hp, 0, 0)),
                  seq_blk, seq_blk, seq_blk],
        out_specs=seq_blk,
        out_shape=jax.ShapeDtypeStruct((t, W_CH), BF16),
        scratch_shapes=[pltpu.VMEM((win, 2 * tq), F32),
                        pltpu.VMEM((vt_rows, seq), BF16),
                        pltpu.VMEM((nq, 2 * tq, 2 * HD_CH), BF16)],
        compiler_params=pltpu.CompilerParams(
            dimension_semantics=("parallel", "arbitrary"), vmem_limit_bytes=_vmem_limit(est)),
        name="chunk_attn",
    )(gv, cq, ck, cv)


def _bias_table_ext(rel_bias, win):
    d = jnp.arange(2 * win) - win
    rel = jnp.clip(d, -REL_FUT_MAX, REL_PAST_MAX) + REL_FUT_MAX
    ext = rel_bias[:, :, rel].astype(F32)
    return ext.reshape(rel_bias.shape[0], H_CH // 2, 2, 2 * win)


def _mix_kernel(x_ref, g_ref, za_ref, ob_ref, oc_ref, gate_ref, wa_ref, wb_ref, wc_ref, wo_ref, o_ref):
    d = D_MODEL
    mix = gate_ref[:, 0:d].astype(F32) * _dot(za_ref[...], wa_ref[...])
    mix = mix + gate_ref[:, d:2 * d].astype(F32) * _dot(ob_ref[...], wb_ref[...])
    mix = mix + gate_ref[:, 2 * d:3 * d].astype(F32) * _dot(oc_ref[...], wc_ref[...])
    y = _dot(mix.astype(BF16), wo_ref[...])
    o_ref[...] = x_ref[...] + _rms(y, g_ref[3:4, :])


def _mix(x, norm_g, za, ob, oc, gates, wa, wb, wc, wo, layer):
    t, d = x.shape
    tm = min(MIX_TM, t)
    est = (2 * 2 * tm * d * 4 + 2 * tm * (3 * C_CONV * 2 + 3 * d * 2)
           + 2 * (3 * C_CONV * d * 2 + d * d * 2) + 4 * tm * d * 4)
    row = lambda i: (i, 0)
    wspec = lambda k: pl.BlockSpec((None, k, d), lambda i: (layer, 0, 0))
    return pl.pallas_call(
        _mix_kernel,
        grid=(t // tm,),
        in_specs=[
            pl.BlockSpec((tm, d), row),
            pl.BlockSpec((None, N_NORMS, d), lambda i: (layer, 0, 0)),
            pl.BlockSpec((tm, C_CONV), row), pl.BlockSpec((tm, W_DIFF), row), pl.BlockSpec((tm, W_CH), row),
            pl.BlockSpec((tm, N_BRANCH * d), row),
            wspec(C_CONV), wspec(W_DIFF), wspec(W_CH), wspec(d),
        ],
        out_specs=pl.BlockSpec((tm, d), row),
        out_shape=jax.ShapeDtypeStruct((t, d), F32),
        compiler_params=pltpu.CompilerParams(
            dimension_semantics=("parallel",), vmem_limit_bytes=_vmem_limit(est)),
        name="mix",
    )(x, norm_g, za, ob, oc, gates, wa, wb, wc, wo)


def _ffn_in_layout(w):
    nl, d, _ = w.shape
    nk = D_FF // FFN_TF
    w = w.astype(BF16).reshape(nl, d, 2, nk, FFN_TF)
    return jnp.transpose(w, (0, 3, 1, 2, 4)).reshape(nl, nk, d, 2 * FFN_TF)


def _rope_tables(seq):
    pos = jnp.arange(seq, dtype=F32)
    inv_freq = ROPE_THETA ** (-jnp.arange(0, HD_DIFF, 2, dtype=F32) / HD_DIFF)
    ang = pos[:, None] * inv_freq[None, :]
    reps = W_DIFF // (HD_DIFF // 2)
    cos_t = jnp.tile(jnp.cos(ang), (1, reps))
    sign = jnp.tile(jnp.concatenate([-jnp.ones(HD_DIFF // 2, F32), jnp.ones(HD_DIFF // 2, F32)]),
                    W_DIFF // HD_DIFF)
    sin_t = jnp.tile(jnp.sin(ang), (1, reps)) * sign[None, :]
    return cos_t, sin_t


def kernel(x, norm_g, ffn1_in, ffn1_out, w_in, gate_b, conv_dw, conv_b, conv_ln, w_conv_out,
           diff_lambda, w_diff_out, rel_bias, w_chunk_out, w_o, ffn2_in, ffn2_out):
    batch, seq, d = x.shape
    depth = norm_g.shape[0]
    assert d == D_MODEL and seq % CH_TQ == 0 and seq % DIFF_TQ == 0 and seq % INPROJ_TM == 0

    ffn1_in_r, ffn2_in_r = _ffn_in_layout(ffn1_in), _ffn_in_layout(ffn2_in)
    ffn1_out_b, ffn2_out_b = ffn1_out.astype(BF16), ffn2_out.astype(BF16)
    w_in_b = w_in.astype(BF16)
    wa_b, wb_b, wc_b, wo_b = (w.astype(BF16) for w in (w_conv_out, w_diff_out, w_chunk_out, w_o))
    gate_b3 = gate_b.reshape(depth, 1, N_BRANCH * d)
    conv_b3 = conv_b.reshape(depth, 1, C_CONV)
    cos_t, sin_t = _rope_tables(seq)
    gv = _bias_table_ext(rel_bias, CH_WIN)

    xt = x.reshape(batch * seq, d)
    for l in range(depth):
        xt = _ffn(xt, norm_g, ffn1_in_r, ffn1_out_b, l, 0, 1)
        z, dq, dk, dv, cq, ck, cv, gates = _inproj(xt, norm_g, w_in_b, gate_b3, cos_t, sin_t, l, seq)
        za = _conv(z, conv_dw, conv_b3, conv_ln, l, batch, seq)
        ob = _diff_attn(dq, dk, dv, diff_lambda, l, batch, seq)
        oc = _chunk_attn(cq, ck, cv, gv, l, batch, seq)
        xt = _mix(xt, norm_g, za, ob, oc, gates, wa_b, wb_b, wc_b, wo_b, l)
        xt = _ffn(xt, norm_g, ffn2_in_r, ffn2_out_b, l, 4, 5)
    return xt.reshape(batch, seq, d)
```

```python
import functools
import math

import jax
import jax.numpy as jnp
from jax import lax
from jax.experimental import pallas as pl
from jax.experimental.pallas import tpu as pltpu

F32 = jnp.float32
BF16 = jnp.bfloat16

D_MODEL = 1024
CHUNK = 64
EPS = 1e-6
NEG_INF = -1e30
LOG2_E = math.log2(math.e)
N_NORMS = 6
D_FF = 2816
FFN_RES = 0.5
ROPE_THETA = 10000.0
C_CONV = D_MODEL // 2
CONV_K = 31
H_DIFF = 4
HD_DIFF = 64
DV_DIFF = 2 * HD_DIFF
W_DIFF = H_DIFF * 2 * HD_DIFF
H_CH = 8
HD_CH = 64
W_CH = H_CH * HD_CH
PAST_CHUNKS = 8
REL_PAST_MAX = 256
REL_FUT_MAX = CHUNK - 1
N_REL = REL_PAST_MAX + REL_FUT_MAX + 1
N_BRANCH = 3
OFF_GLU = 0
OFF_DQ = OFF_GLU + 2 * C_CONV
OFF_DK = OFF_DQ + W_DIFF
OFF_DV = OFF_DK + W_DIFF
OFF_CQ = OFF_DV + W_DIFF
OFF_CK = OFF_CQ + W_CH
OFF_CV = OFF_CK + W_CH
OFF_GATE = OFF_CV + W_CH
D_IN = OFF_GATE + N_BRANCH * D_MODEL

LANES = 128
SUBLANES = 8
MXU_DIM = 256
VMEM_BYTES = 64 * 1024 * 1024
MIB = 1024 * 1024

FFN_TM = 512
FFN_TF = 2 * MXU_DIM
FFN_SUB = 2
INPROJ_TM = 512
INPROJ_SUB = 2
MIX_SUB = 2
CONV_TS = 256
CONV_HALO = 32
CONV_SUB = 32
DIFF_TQ = 256
DIFF_VT_ROWS = DV_DIFF + 16
DIFF_LOOKAHEAD = 2
CH_TQ = 256
CH_WIN = CH_TQ + PAST_CHUNKS * CHUNK
CH_LOOKAHEAD = 2
MIX_TM = 512


def _vmem_limit(nbytes):
    return int(min(nbytes + 16 * MIB, VMEM_BYTES - 8 * MIB))


def _rms(x, g):
    return x * lax.rsqrt(jnp.mean(x * x, axis=-1, keepdims=True) + EPS) * g


def _sigmoid(x):
    return 1.0 / (1.0 + jnp.exp(-x))


def _dot(a, b):
    return jnp.dot(a, b, preferred_element_type=F32)


def _dot_nt(a, b):
    return lax.dot_general(a, b, (((1,), (1,)), ((), ())), preferred_element_type=F32)


def _ffn_kernel(x_ref, g_ref, win_ref, wout_ref, o_ref, act_scr, *, g_pre, g_post):
    tm = x_ref.shape[0]
    ts = tm // FFN_SUB
    for s in range(FFN_SUB):
        rows = slice(s * ts, (s + 1) * ts)
        h = _rms(x_ref[rows, :], g_ref[g_pre:g_pre + 1, :]).astype(BF16)
        c0 = 0
        while c0 < D_FF:
            cw = min(FFN_TF, D_FF - c0)
            gate = _dot(h, win_ref[:, c0:c0 + cw])
            up = _dot(h, win_ref[:, D_FF + c0:D_FF + c0 + cw])
            act_scr[rows, c0:c0 + cw] = (gate * _sigmoid(gate) * up).astype(BF16)
            c0 += cw
        y = _dot(act_scr[rows, :], wout_ref[...])
        o_ref[rows, :] = x_ref[rows, :] + FFN_RES * _rms(y, g_ref[g_post:g_post + 1, :])


def _ffn(x, norm_g, w_in, w_out, layer, g_pre, g_post):
    t, d = x.shape
    tm = min(FFN_TM, t)
    est = (2 * 2 * tm * d * 4 + d * 2 * D_FF * 2 + D_FF * d * 2 + tm * D_FF * 2
           + tm * d * 2 + 3 * tm * 2 * FFN_TF * 4 + 2 * tm * d * 4)
    once = pl.Buffered(1)
    return pl.pallas_call(
        functools.partial(_ffn_kernel, g_pre=g_pre, g_post=g_post),
        grid=(t // tm,),
        in_specs=[
            pl.BlockSpec((tm, d), lambda i: (i, 0)),
            pl.BlockSpec((None, N_NORMS, d), lambda i: (layer, 0, 0)),
            pl.BlockSpec((None, d, 2 * D_FF), lambda i: (layer, 0, 0), pipeline_mode=once),
            pl.BlockSpec((None, D_FF, d), lambda i: (layer, 0, 0), pipeline_mode=once),
        ],
        out_specs=pl.BlockSpec((tm, d), lambda i: (i, 0)),
        out_shape=jax.ShapeDtypeStruct((t, d), F32),
        scratch_shapes=[pltpu.VMEM((tm, D_FF), BF16)],
        compiler_params=pltpu.CompilerParams(
            dimension_semantics=("parallel",), vmem_limit_bytes=_vmem_limit(est)),
        name="ffn",
    )(x, norm_g, w_in, w_out)


def _inproj_kernel(x_ref, g_ref, w_ref, gb_ref, cos_ref, sin_ref,
                   z_ref, dq_ref, dk_ref, dv_ref, cq_ref, ck_ref, cv_ref, gate_ref):
    lane = lax.broadcasted_iota(jnp.int32, (1, W_DIFF), 1)
    first_half = (lane % HD_DIFF) < (HD_DIFF // 2)
    ts = x_ref.shape[0] // INPROJ_SUB
    for s in range(INPROJ_SUB):
        rows = slice(s * ts, (s + 1) * ts)
        h = _rms(x_ref[rows, :], g_ref[2:3, :]).astype(BF16)

        def proj(off, width, h=h):
            return _dot(h, w_ref[:, off:off + width])

        def rope(t, rows=rows):
            partner = jnp.where(first_half,
                                pltpu.roll(t, W_DIFF - HD_DIFF // 2, 1),
                                pltpu.roll(t, HD_DIFF // 2, 1))
            return t * cos_ref[rows, :] + partner * sin_ref[rows, :]

        for j in range(N_BRANCH):
            lo = j * D_MODEL
            pg = proj(OFF_GATE + lo, D_MODEL) + gb_ref[:, lo:lo + D_MODEL]
            gate_ref[rows, lo:lo + D_MODEL] = _sigmoid(pg).astype(BF16)
        u = proj(OFF_GLU, 2 * C_CONV)
        z_ref[rows, :] = u[:, :C_CONV] * _sigmoid(u[:, C_CONV:])
        dq_ref[rows, :] = (rope(proj(OFF_DQ, W_DIFF)) * (HD_DIFF ** -0.5 * LOG2_E)).astype(BF16)
        dk_ref[rows, :] = rope(proj(OFF_DK, W_DIFF)).astype(BF16)
        cq_ref[rows, :] = (proj(OFF_CQ, W_CH) * (HD_CH ** -0.5 * LOG2_E)).astype(BF16)
        dv_ref[rows, :] = proj(OFF_DV, W_DIFF).astype(BF16)
        ck_ref[rows, :] = proj(OFF_CK, W_CH).astype(BF16)
        cv_ref[rows, :] = proj(OFF_CV, W_CH).astype(BF16)


def _inproj(x, norm_g, w_in, gate_b, cos_t, sin_t, layer, seq):
    t, d = x.shape
    tm = min(INPROJ_TM, seq)
    ns = seq // tm
    est = (d * D_IN * 2 + 2 * tm * d * 4 + 2 * tm * (C_CONV * 4 + 6 * W_DIFF * 2 + 3 * d * 2)
           + 2 * 2 * tm * W_DIFF * 4 + 4 * tm * d * 4)
    row = lambda i: (i, 0)
    half = lambda dt: jax.ShapeDtypeStruct((t, W_DIFF), dt)
    return pl.pallas_call(
        _inproj_kernel,
        grid=(t // tm,),
        in_specs=[
            pl.BlockSpec((tm, d), row),
            pl.BlockSpec((None, N_NORMS, d), lambda i: (layer, 0, 0)),
            pl.BlockSpec((None, d, D_IN), lambda i: (layer, 0, 0), pipeline_mode=pl.Buffered(1)),
            pl.BlockSpec((None, 1, N_BRANCH * d), lambda i: (layer, 0, 0)),
            pl.BlockSpec((tm, W_DIFF), lambda i: (i % ns, 0)),
            pl.BlockSpec((tm, W_DIFF), lambda i: (i % ns, 0)),
        ],
        out_specs=[pl.BlockSpec((tm, C_CONV), row)] + [pl.BlockSpec((tm, W_DIFF), row)] * 6
        + [pl.BlockSpec((tm, N_BRANCH * d), row)],
        out_shape=[jax.ShapeDtypeStruct((t, C_CONV), F32)] + [half(BF16)] * 6
        + [jax.ShapeDtypeStruct((t, N_BRANCH * d), BF16)],
        compiler_params=pltpu.CompilerParams(
            dimension_semantics=("parallel",), vmem_limit_bytes=_vmem_limit(est)),
        name="inproj",
    )(x, norm_g, w_in, gate_b, cos_t, sin_t)


def _conv_kernel(z_ref, halo_ref, dw_ref, db_ref, ln_ref, o_ref, win_scr, y_scr, *, ts):
    j = pl.program_id(1)
    nlt = C_CONV // LANES
    halo = jnp.where(j == 0, 0.0, halo_ref[...])
    for c in range(nlt):
        win_scr[c, 0:CONV_HALO, :] = halo[:, c * LANES:(c + 1) * LANES]
        win_scr[c, CONV_HALO:, :] = z_ref[:, c * LANES:(c + 1) * LANES]
    lead = CONV_HALO - (CONV_K - 1)
    nsub = ts // CONV_SUB

    def sub_step(sub, carry):
        base = pl.multiple_of(sub * CONV_SUB, CONV_SUB)
        for c in range(nlt):
            view = win_scr.at[c, pl.ds(base, CONV_SUB + CONV_HALO), :]
            acc = jnp.zeros((CONV_SUB, LANES), F32)
            for k in range(CONV_K):
                tap = dw_ref[k:k + 1, c * LANES:(c + 1) * LANES]
                acc = acc + tap * view[lead + k:lead + k + CONV_SUB, :]
            y_scr[pl.ds(base, CONV_SUB), c * LANES:(c + 1) * LANES] = acc
        return carry

    lax.fori_loop(0, nsub, sub_step, 0)
    y = y_scr[...] + db_ref[...]
    mu = jnp.mean(y, axis=-1, keepdims=True)
    yc = y - mu
    var = jnp.mean(yc * yc, axis=-1, keepdims=True)
    yn = yc * lax.rsqrt(var + EPS) * ln_ref[0:1, :] + ln_ref[1:2, :]
    o_ref[...] = (yn * _sigmoid(yn)).astype(BF16)


def _conv(z, conv_dw, conv_b, conv_ln, layer, batch, seq):
    t = z.shape[0]
    ts = min(CONV_TS, seq)
    ns = seq // ts
    hb = ts // CONV_HALO
    est = 2 * (ts + CONV_HALO) * C_CONV * 4 + (ts + CONV_HALO) * C_CONV * 4 + 4 * ts * C_CONV * 4
    return pl.pallas_call(
        functools.partial(_conv_kernel, ts=ts),
        grid=(batch, ns),
        in_specs=[
            pl.BlockSpec((ts, C_CONV), lambda b, j: (b * ns + j, 0)),
            pl.BlockSpec((CONV_HALO, C_CONV), lambda b, j: (jnp.maximum((b * ns + j) * hb - 1, 0), 0)),
            pl.BlockSpec((None, CONV_K, C_CONV), lambda b, j: (layer, 0, 0)),
            pl.BlockSpec((None, 1, C_CONV), lambda b, j: (layer, 0, 0)),
            pl.BlockSpec((None, 2, C_CONV), lambda b, j: (layer, 0, 0)),
        ],
        out_specs=pl.BlockSpec((ts, C_CONV), lambda b, j: (b * ns + j, 0)),
        out_shape=jax.ShapeDtypeStruct((t, C_CONV), BF16),
        scratch_shapes=[pltpu.VMEM((C_CONV // LANES, ts + CONV_HALO, LANES), F32),
                        pltpu.VMEM((ts, C_CONV), F32)],
        compiler_params=pltpu.CompilerParams(
            dimension_semantics=("parallel", "parallel"), vmem_limit_bytes=_vmem_limit(est)),
        name="conv",
    )(z, z, conv_dw, conv_b, conv_ln)


def _diff_kernel(lam_ref, q_ref, k_ref, v_ref, o_ref, vt_scr, qq_scr, m_scr, acc_scr,
                 *, tq, nq, lam_init):
    row = lax.broadcasted_iota(jnp.int32, (DIFF_VT_ROWS - DV_DIFF, tq), 0)
    ones_row = jnp.where(row == 0, 1.0, 0.0).astype(BF16)
    for kb in range(nq):
        vt_scr[kb, 0:DV_DIFF, :] = v_ref[kb * tq:(kb + 1) * tq, :].T
        vt_scr[kb, DV_DIFF:, :] = ones_row

    lamp = lam_ref[...]
    lam = (jnp.exp(jnp.sum(lamp[0:1, :] * lamp[1:2, :], keepdims=True))
           - jnp.exp(jnp.sum(lamp[2:3, :] * lamp[3:4, :], keepdims=True)) + lam_init)

    lane = lax.broadcasted_iota(jnp.int32, (1, DV_DIFF), 1)
    comp0 = lane < HD_DIFF
    for qb in range(nq):
        q = q_ref[qb * tq:(qb + 1) * tq, :]
        zero = jnp.zeros_like(q)
        qq_scr[qb, 0:tq, :] = jnp.where(comp0, q, zero)
        qq_scr[qb, tq:2 * tq, :] = jnp.where(comp0, zero, q)

    m_scr[...] = jnp.full(m_scr.shape, -jnp.inf, F32)
    acc_scr[...] = jnp.zeros(acc_scr.shape, F32)

    kc = lax.broadcasted_iota(jnp.int32, (tq, 2 * tq), 0) // CHUNK
    qc = lax.broadcasted_iota(jnp.int32, (tq, 2 * tq), 1) % tq // CHUNK
    visible = kc <= qc

    def scores(qb, kb):
        st = _dot_nt(k_ref[kb * tq:(kb + 1) * tq, :], qq_scr[qb])
        return jnp.where(visible, st, NEG_INF) if kb == qb else st

    def update(qb, kb, st):
        m_old = m_scr[qb]
        m_new = jnp.maximum(m_old, jnp.max(st, axis=0, keepdims=True))
        p = jnp.exp2(st - m_new)
        alpha = jnp.exp2(m_old - m_new)
        acc_scr[qb] = alpha * acc_scr[qb] + _dot(vt_scr[kb], p.astype(BF16))
        m_scr[qb] = m_new

    def finish(qb):
        on = acc_scr[qb, 0:DV_DIFF, :] * (1.0 / acc_scr[qb, DV_DIFF:DV_DIFF + 1, :])
        ot = on[:, :tq] - lam * on[:, tq:]
        ot = ot * lax.rsqrt(jnp.mean(ot * ot, axis=0, keepdims=True) + EPS) * (1.0 - lam_init)
        o_ref[qb * tq:(qb + 1) * tq, :] = ot.T.astype(BF16)

    steps = []
    for a in range(nq // 2):
        b = nq - 1 - a
        seq_a = [(a, a)] + [(a, kb) for kb in range(a)]
        seq_b = [(b, b)] + [(b, kb) for kb in range(b)]
        for n in range(max(len(seq_a), len(seq_b))):
            steps += seq_a[n:n + 1] + seq_b[n:n + 1]
    last = {qb: max(n for n, s in enumerate(steps) if s[0] == qb) for qb in range(nq)}
    pending = [scores(*s) for s in steps[:DIFF_LOOKAHEAD]]
    for n, (qb, kb) in enumerate(steps):
        if n + DIFF_LOOKAHEAD < len(steps):
            pending.append(scores(*steps[n + DIFF_LOOKAHEAD]))
        update(qb, kb, pending.pop(0))
        if last[qb] == n:
            finish(qb)


def _diff_attn(dq, dk, dv, diff_lambda, layer, batch, seq):
    t = dq.shape[0]
    tq = min(DIFF_TQ, seq)
    nq = seq // tq
    assert nq % 2 == 0
    lam_init = 0.8 - 0.6 * math.exp(-0.3 * layer)
    est = (2 * 4 * seq * DV_DIFF * 2 + nq * DIFF_VT_ROWS * tq * 2 + nq * 2 * tq * DV_DIFF * 2
           + nq * (DIFF_VT_ROWS + 8) * 2 * tq * 4 + (DIFF_LOOKAHEAD + 4) * tq * 2 * tq * 4)
    seq_blk = pl.BlockSpec((seq, DV_DIFF), lambda b, h: (b, h))
    return pl.pallas_call(
        functools.partial(_diff_kernel, tq=tq, nq=nq, lam_init=lam_init),
        grid=(batch, H_DIFF),
        in_specs=[pl.BlockSpec((None, 4, HD_DIFF), lambda b, h: (layer, 0, 0)), seq_blk, seq_blk, seq_blk],
        out_specs=seq_blk,
        out_shape=jax.ShapeDtypeStruct((t, W_DIFF), BF16),
        scratch_shapes=[pltpu.VMEM((nq, DIFF_VT_ROWS, tq), BF16),
                        pltpu.VMEM((nq, 2 * tq, DV_DIFF), BF16),
                        pltpu.VMEM((nq, 1, 2 * tq), F32),
                        pltpu.VMEM((nq, DIFF_VT_ROWS, 2 * tq), F32)],
        compiler_params=pltpu.CompilerParams(
            dimension_semantics=("parallel", "parallel"), vmem_limit_bytes=_vmem_limit(est)),
        name="diff_attn",
    )(diff_lambda, dq, dk, dv)


def _chunk_kernel(gv_ref, q_ref, k_ref, v_ref, o_ref, bmt_scr, vt_scr, qq_scr, *, tq, nq, win):
    past = win - tq

    @pl.when(pl.program_id(1) == 0)
    def _():
        rc = lax.broadcasted_iota(jnp.int32, (win, tq), 0) // CHUNK
        jc = lax.broadcasted_iota(jnp.int32, (win, tq), 1) // CHUNK
        in_band = (rc >= jc) & (rc <= jc + PAST_CHUNKS)
        for e in range(2):
            ext = jnp.broadcast_to(gv_ref[e:e + 1, :], (win, 2 * win))
            toep = pltpu.roll(ext, tq, 1, stride=1, stride_axis=0)[:, :tq]
            bmt_scr[:, e * tq:(e + 1) * tq] = jnp.where(in_band, toep * LOG2_E, NEG_INF)

    vt_scr[0:2 * HD_CH, :] = v_ref[...].T
    row = lax.broadcasted_iota(jnp.int32, (vt_scr.shape[0] - 2 * HD_CH, vt_scr.shape[1]), 0)
    vt_scr[2 * HD_CH:, :] = jnp.where(row == 0, 1.0, 0.0).astype(BF16)

    lane = lax.broadcasted_iota(jnp.int32, (1, 2 * HD_CH), 1)
    head0 = lane < HD_CH
    for qb in range(nq):
        q = q_ref[qb * tq:(qb + 1) * tq, :]
        zero = jnp.zeros_like(q)
        qq_scr[qb, 0:tq, :] = jnp.where(head0, q, zero)
        qq_scr[qb, tq:2 * tq, :] = jnp.where(head0, zero, q)

    def window(qb):
        ws = max(qb * tq - past, 0)
        return ws, (qb + 1) * tq - ws

    def scores(qb):
        ws, wl = window(qb)
        return _dot_nt(k_ref[ws:ws + wl, :], qq_scr[qb]) + bmt_scr[win - wl:win, :]

    def finish(qb, st):
        ws, wl = window(qb)
        p = jnp.exp2(st - jnp.max(st, axis=0, keepdims=True))
        acc = _dot(vt_scr[:, ws:ws + wl], p.astype(BF16))
        den = acc[2 * HD_CH:2 * HD_CH + 1, :]
        o0 = acc[0:HD_CH, 0:tq] * (1.0 / den[:, 0:tq])
        o1 = acc[HD_CH:2 * HD_CH, tq:2 * tq] * (1.0 / den[:, tq:2 * tq])
        o_ref[qb * tq:(qb + 1) * tq, :] = jnp.concatenate([o0, o1], axis=0).T.astype(BF16)

    pending = [scores(qb) for qb in range(min(CH_LOOKAHEAD, nq))]
    for qb in range(nq):
        if qb + CH_LOOKAHEAD < nq:
            pending.append(scores(qb + CH_LOOKAHEAD))
        finish(qb, pending.pop(0))


def _chunk_attn(cq, ck, cv, gv, layer, batch, seq):
    t = cq.shape[0]
    tq = min(CH_TQ, seq)
    nq = seq // tq
    win = CH_WIN
    assert seq >= win
    vt_rows = 2 * HD_CH + 16
    est = (win * 2 * tq * 4 + vt_rows * seq * 2 + nq * 2 * tq * 2 * HD_CH * 2 + 2 * 4 * seq * 2 * HD_CH * 2
           + win * 2 * win * 4 * 3 + (CH_LOOKAHEAD + 4) * win * 2 * tq * 4)
    seq_blk = pl.BlockSpec((seq, 2 * HD_CH), lambda hp, b: (b, hp))
    return pl.pallas_call(
        functools.partial(_chunk_kernel, tq=tq, nq=nq, win=win),
        grid=(H_CH // 2, batch),
        in_specs=[pl.BlockSpec((None, None, 2, 2 * win), lambda hp, b: (layer, hp, 0, 0)),
                  seq_blk, seq_blk, seq_blk],
        out_specs=seq_blk,
        out_shape=jax.ShapeDtypeStruct((t, W_CH), BF16),
        scratch_shapes=[pltpu.VMEM((win, 2 * tq), F32),
                        pltpu.VMEM((vt_rows, seq), BF16),
                        pltpu.VMEM((nq, 2 * tq, 2 * HD_CH), BF16)],
        compiler_params=pltpu.CompilerParams(
            dimension_semantics=("parallel", "arbitrary"), vmem_limit_bytes=_vmem_limit(est)),
        name="chunk_attn",
    )(gv, cq, ck, cv)


def _bias_table_ext(rel_bias, win):
    d = jnp.arange(2 * win) - win
    rel = jnp.clip(d, -REL_FUT_MAX, REL_PAST_MAX) + REL_FUT_MAX
    ext = rel_bias[:, :, rel].astype(F32)
    return ext.reshape(rel_bias.shape[0], H_CH // 2, 2, 2 * win)


def _mix_kernel(x_ref, g_ref, za_ref, ob_ref, oc_ref, gate_ref, wa_ref, wb_ref, wc_ref, wo_ref, o_ref):
    d = D_MODEL
    ts = x_ref.shape[0] // MIX_SUB
    tiles = [slice(s * ts, (s + 1) * ts) for s in range(MIX_SUB)]
    mixes = []
    for rows in tiles:
        mix = gate_ref[rows, 0:d].astype(F32) * _dot(za_ref[rows, :], wa_ref[...])
        mix = mix + gate_ref[rows, d:2 * d].astype(F32) * _dot(ob_ref[rows, :], wb_ref[...])
        mix = mix + gate_ref[rows, 2 * d:3 * d].astype(F32) * _dot(oc_ref[rows, :], wc_ref[...])
        mixes.append(mix.astype(BF16))
    for rows, mix in zip(tiles, mixes):
        y = _dot(mix, wo_ref[...])
        o_ref[rows, :] = x_ref[rows, :] + _rms(y, g_ref[3:4, :])


def _mix(x, norm_g, za, ob, oc, gates, wa, wb, wc, wo, layer):
    t, d = x.shape
    tm = min(MIX_TM, t)
    est = (2 * 2 * tm * d * 4 + 2 * tm * (3 * C_CONV * 2 + 3 * d * 2)
           + 2 * (3 * C_CONV * d * 2 + d * d * 2) + 4 * tm * d * 4)
    row = lambda i: (i, 0)
    wspec = lambda k: pl.BlockSpec((None, k, d), lambda i: (layer, 0, 0))
    return pl.pallas_call(
        _mix_kernel,
        grid=(t // tm,),
        in_specs=[
            pl.BlockSpec((tm, d), row),
            pl.BlockSpec((None, N_NORMS, d), lambda i: (layer, 0, 0)),
            pl.BlockSpec((tm, C_CONV), row), pl.BlockSpec((tm, W_DIFF), row), pl.BlockSpec((tm, W_CH), row),
            pl.BlockSpec((tm, N_BRANCH * d), row),
            wspec(C_CONV), wspec(W_DIFF), wspec(W_CH), wspec(d),
        ],
        out_specs=pl.BlockSpec((tm, d), row),
        out_shape=jax.ShapeDtypeStruct((t, d), F32),
        compiler_params=pltpu.CompilerParams(
            dimension_semantics=("parallel",), vmem_limit_bytes=_vmem_limit(est)),
        name="mix",
    )(x, norm_g, za, ob, oc, gates, wa, wb, wc, wo)


def _rope_tables(seq):
    pos = jnp.arange(seq, dtype=F32)
    inv_freq = ROPE_THETA ** (-jnp.arange(0, HD_DIFF, 2, dtype=F32) / HD_DIFF)
    ang = pos[:, None] * inv_freq[None, :]
    reps = W_DIFF // (HD_DIFF // 2)
    cos_t = jnp.tile(jnp.cos(ang), (1, reps))
    sign = jnp.tile(jnp.concatenate([-jnp.ones(HD_DIFF // 2, F32), jnp.ones(HD_DIFF // 2, F32)]),
                    W_DIFF // HD_DIFF)
    sin_t = jnp.tile(jnp.sin(ang), (1, reps)) * sign[None, :]
    return cos_t, sin_t


def kernel(x, norm_g, ffn1_in, ffn1_out, w_in, gate_b, conv_dw, conv_b, conv_ln, w_conv_out,
           diff_lambda, w_diff_out, rel_bias, w_chunk_out, w_o, ffn2_in, ffn2_out):
    batch, seq, d = x.shape
    depth = norm_g.shape[0]
    assert d == D_MODEL and seq % CH_TQ == 0 and seq % DIFF_TQ == 0 and seq % INPROJ_TM == 0

    ffn1_in_b, ffn2_in_b = ffn1_in.astype(BF16), ffn2_in.astype(BF16)
    ffn1_out_b, ffn2_out_b = ffn1_out.astype(BF16), ffn2_out.astype(BF16)
    w_in_b = w_in.astype(BF16)
    wa_b, wb_b, wc_b, wo_b = (w.astype(BF16) for w in (w_conv_out, w_diff_out, w_chunk_out, w_o))
    gate_b3 = gate_b.reshape(depth, 1, N_BRANCH * d)
    conv_b3 = conv_b.reshape(depth, 1, C_CONV)
    cos_t, sin_t = _rope_tables(seq)
    gv = _bias_table_ext(rel_bias, CH_WIN)

    xt = x.reshape(batch * seq, d)
    for l in range(depth):
        xt = _ffn(xt, norm_g, ffn1_in_b, ffn1_out_b, l, 0, 1)
        z, dq, dk, dv, cq, ck, cv, gates = _inproj(xt, norm_g, w_in_b, gate_b3, cos_t, sin_t, l, seq)
        za = _conv(z, conv_dw, conv_b3, conv_ln, l, batch, seq)
        ob = _diff_attn(dq, dk, dv, diff_lambda, l, batch, seq)
        oc = _chunk_attn(cq, ck, cv, gv, l, batch, seq)
        xt = _mix(xt, norm_g, za, ob, oc, gates, wa_b, wb_b, wc_b, wo_b, l)
        xt = _ffn(xt, norm_g, ffn2_in_b, ffn2_out_b, l, 4, 5)
    return xt.reshape(batch, seq, d)
```

```python
import functools
import math

import jax
import jax.numpy as jnp
from jax import lax
from jax.experimental import pallas as pl
from jax.experimental.pallas import tpu as pltpu

F32 = jnp.float32
BF16 = jnp.bfloat16

D_MODEL = 1024
CHUNK = 64
EPS = 1e-6
NEG_INF = -1e30
LOG2_E = math.log2(math.e)
N_NORMS = 6
D_FF = 2816
FFN_RES = 0.5
ROPE_THETA = 10000.0
C_CONV = D_MODEL // 2
CONV_K = 31
H_DIFF = 4
HD_DIFF = 64
DV_DIFF = 2 * HD_DIFF
W_DIFF = H_DIFF * 2 * HD_DIFF
H_CH = 8
HD_CH = 64
W_CH = H_CH * HD_CH
PAST_CHUNKS = 8
REL_PAST_MAX = 256
REL_FUT_MAX = CHUNK - 1
N_REL = REL_PAST_MAX + REL_FUT_MAX + 1
N_BRANCH = 3
OFF_GLU = 0
OFF_DQ = OFF_GLU + 2 * C_CONV
OFF_DK = OFF_DQ + W_DIFF
OFF_DV = OFF_DK + W_DIFF
OFF_CQ = OFF_DV + W_DIFF
OFF_CK = OFF_CQ + W_CH
OFF_CV = OFF_CK + W_CH
OFF_GATE = OFF_CV + W_CH
D_IN = OFF_GATE + N_BRANCH * D_MODEL

LANES = 128
SUBLANES = 8
MXU_DIM = 256
VMEM_BYTES = 64 * 1024 * 1024
MIB = 1024 * 1024

FFN_TM = 512
FFN_TF = 2 * MXU_DIM
FFN_SUB = 2
INPROJ_TM = 512
INPROJ_SUB = 2
MIX_SUB = 4
CONV_TS = 512
CONV_HALO = 32
CONV_SUB = 32
DIFF_TQ = 256
DIFF_HEADS = 2
DIFF_VT_ROWS = DV_DIFF + 16
DIFF_LOOKAHEAD = 2
CH_TQ = 256
CH_WIN = CH_TQ + PAST_CHUNKS * CHUNK
CH_LOOKAHEAD = 2
CH_VT_ROWS = HD_CH + 16
MIX_TM = 1024


def _vmem_limit(nbytes):
    return int(min(nbytes + 16 * MIB, VMEM_BYTES - 8 * MIB))


def _rms(x, g):
    return x * lax.rsqrt(jnp.mean(x * x, axis=-1, keepdims=True) + EPS) * g


def _sigmoid(x):
    return 1.0 / (1.0 + jnp.exp(-x))


def _dot(a, b):
    return jnp.dot(a, b, preferred_element_type=F32)


def _dot_nt(a, b):
    return lax.dot_general(a, b, (((1,), (1,)), ((), ())), preferred_element_type=F32)


def _ffn_kernel(x_ref, g_ref, win_ref, wout_ref, o_ref, act_scr, *, g_pre, g_post):
    tm = x_ref.shape[0]
    ts = tm // FFN_SUB
    for s in range(FFN_SUB):
        rows = slice(s * ts, (s + 1) * ts)
        h = _rms(x_ref[rows, :], g_ref[g_pre:g_pre + 1, :]).astype(BF16)
        c0 = 0
        while c0 < D_FF:
            cw = min(FFN_TF, D_FF - c0)
            gate = _dot(h, win_ref[:, c0:c0 + cw])
            up = _dot(h, win_ref[:, D_FF + c0:D_FF + c0 + cw])
            act_scr[rows, c0:c0 + cw] = (gate * _sigmoid(gate) * up).astype(BF16)
            c0 += cw
        y = _dot(act_scr[rows, :], wout_ref[...])
        o_ref[rows, :] = x_ref[rows, :] + FFN_RES * _rms(y, g_ref[g_post:g_post + 1, :])


def _ffn(x, norm_g, w_in, w_out, layer, g_pre, g_post):
    t, d = x.shape
    tm = min(FFN_TM, t)
    est = (2 * 2 * tm * d * 4 + d * 2 * D_FF * 2 + D_FF * d * 2 + tm * D_FF * 2
           + tm * d * 2 + 3 * tm * 2 * FFN_TF * 4 + 2 * tm * d * 4)
    once = pl.Buffered(1)
    return pl.pallas_call(
        functools.partial(_ffn_kernel, g_pre=g_pre, g_post=g_post),
        grid=(t // tm,),
        in_specs=[
            pl.BlockSpec((tm, d), lambda i: (i, 0)),
            pl.BlockSpec((None, N_NORMS, d), lambda i: (layer, 0, 0)),
            pl.BlockSpec((None, d, 2 * D_FF), lambda i: (layer, 0, 0), pipeline_mode=once),
            pl.BlockSpec((None, D_FF, d), lambda i: (layer, 0, 0), pipeline_mode=once),
        ],
        out_specs=pl.BlockSpec((tm, d), lambda i: (i, 0)),
        out_shape=jax.ShapeDtypeStruct((t, d), F32),
        scratch_shapes=[pltpu.VMEM((tm, D_FF), BF16)],
        compiler_params=pltpu.CompilerParams(
            dimension_semantics=("parallel",), vmem_limit_bytes=_vmem_limit(est)),
        name="ffn",
    )(x, norm_g, w_in, w_out)


def _inproj_kernel(x_ref, g_ref, w_ref, gb_ref, cos_ref, sin_ref,
                   z_ref, dq_ref, dk_ref, dv_ref, cq_ref, ck_ref, cv_ref, gate_ref):
    lane = lax.broadcasted_iota(jnp.int32, (1, W_DIFF), 1)
    first_half = (lane % HD_DIFF) < (HD_DIFF // 2)
    ts = x_ref.shape[0] // INPROJ_SUB
    for s in range(INPROJ_SUB):
        rows = slice(s * ts, (s + 1) * ts)
        h = _rms(x_ref[rows, :], g_ref[2:3, :]).astype(BF16)

        def proj(off, width, h=h):
            return _dot(h, w_ref[:, off:off + width])

        def rope(t, rows=rows):
            partner = jnp.where(first_half,
                                pltpu.roll(t, W_DIFF - HD_DIFF // 2, 1),
                                pltpu.roll(t, HD_DIFF // 2, 1))
            return t * cos_ref[rows, :] + partner * sin_ref[rows, :]

        for j in range(N_BRANCH):
            lo = j * D_MODEL
            pg = proj(OFF_GATE + lo, D_MODEL) + gb_ref[:, lo:lo + D_MODEL]
            gate_ref[rows, lo:lo + D_MODEL] = _sigmoid(pg).astype(BF16)
        u = proj(OFF_GLU, 2 * C_CONV)
        z_ref[rows, :] = u[:, :C_CONV] * _sigmoid(u[:, C_CONV:])
        dq_ref[rows, :] = (rope(proj(OFF_DQ, W_DIFF)) * (HD_DIFF ** -0.5 * LOG2_E)).astype(BF16)
        dk_ref[rows, :] = rope(proj(OFF_DK, W_DIFF)).astype(BF16)
        cq_ref[rows, :] = (proj(OFF_CQ, W_CH) * (HD_CH ** -0.5 * LOG2_E)).astype(BF16)
        dv_ref[rows, :] = proj(OFF_DV, W_DIFF).astype(BF16)
        ck_ref[rows, :] = proj(OFF_CK, W_CH).astype(BF16)
        cv_ref[rows, :] = proj(OFF_CV, W_CH).astype(BF16)


def _inproj(x, norm_g, w_in, gate_b, cos_t, sin_t, layer, seq):
    t, d = x.shape
    tm = min(INPROJ_TM, seq)
    ns = seq // tm
    est = (d * D_IN * 2 + 2 * tm * d * 4 + 2 * tm * (C_CONV * 4 + 6 * W_DIFF * 2 + 3 * d * 2)
           + 2 * 2 * tm * W_DIFF * 4 + 4 * tm * d * 4)
    row = lambda i: (i, 0)
    half = lambda dt: jax.ShapeDtypeStruct((t, W_DIFF), dt)
    return pl.pallas_call(
        _inproj_kernel,
        grid=(t // tm,),
        in_specs=[
            pl.BlockSpec((tm, d), row),
            pl.BlockSpec((None, N_NORMS, d), lambda i: (layer, 0, 0)),
            pl.BlockSpec((None, d, D_IN), lambda i: (layer, 0, 0), pipeline_mode=pl.Buffered(1)),
            pl.BlockSpec((None, 1, N_BRANCH * d), lambda i: (layer, 0, 0)),
            pl.BlockSpec((tm, W_DIFF), lambda i: (i % ns, 0)),
            pl.BlockSpec((tm, W_DIFF), lambda i: (i % ns, 0)),
        ],
        out_specs=[pl.BlockSpec((tm, C_CONV), row)] + [pl.BlockSpec((tm, W_DIFF), row)] * 6
        + [pl.BlockSpec((tm, N_BRANCH * d), row)],
        out_shape=[jax.ShapeDtypeStruct((t, C_CONV), F32)] + [half(BF16)] * 6
        + [jax.ShapeDtypeStruct((t, N_BRANCH * d), BF16)],
        compiler_params=pltpu.CompilerParams(
            dimension_semantics=("parallel",), vmem_limit_bytes=_vmem_limit(est)),
        name="inproj",
    )(x, norm_g, w_in, gate_b, cos_t, sin_t)


def _conv_kernel(z_ref, halo_ref, dw_ref, db_ref, ln_ref, o_ref, win_scr, y_scr, *, ts):
    j = pl.program_id(1)
    nlt = C_CONV // LANES
    halo = jnp.where(j == 0, 0.0, halo_ref[...])
    for c in range(nlt):
        win_scr[c, 0:CONV_HALO, :] = halo[:, c * LANES:(c + 1) * LANES]
        win_scr[c, CONV_HALO:, :] = z_ref[:, c * LANES:(c + 1) * LANES]
    lead = CONV_HALO - (CONV_K - 1)
    nsub = ts // CONV_SUB

    def sub_step(sub, carry):
        base = pl.multiple_of(sub * CONV_SUB, CONV_SUB)
        for c in range(nlt):
            view = win_scr.at[c, pl.ds(base, CONV_SUB + CONV_HALO), :]
            acc = jnp.zeros((CONV_SUB, LANES), F32)
            for k in range(CONV_K):
                tap = dw_ref[k:k + 1, c * LANES:(c + 1) * LANES]
                acc = acc + tap * view[lead + k:lead + k + CONV_SUB, :]
            y_scr[pl.ds(base, CONV_SUB), c * LANES:(c + 1) * LANES] = acc
        return carry

    lax.fori_loop(0, nsub, sub_step, 0)
    y = y_scr[...] + db_ref[...]
    mu = jnp.mean(y, axis=-1, keepdims=True)
    yc = y - mu
    var = jnp.mean(yc * yc, axis=-1, keepdims=True)
    yn = yc * lax.rsqrt(var + EPS) * ln_ref[0:1, :] + ln_ref[1:2, :]
    o_ref[...] = (yn * _sigmoid(yn)).astype(BF16)


def _conv(z, conv_dw, conv_b, conv_ln, layer, batch, seq):
    t = z.shape[0]
    ts = min(CONV_TS, seq)
    ns = seq // ts
    hb = ts // CONV_HALO
    est = 2 * (ts + CONV_HALO) * C_CONV * 4 + (ts + CONV_HALO) * C_CONV * 4 + 4 * ts * C_CONV * 4
    return pl.pallas_call(
        functools.partial(_conv_kernel, ts=ts),
        grid=(batch, ns),
        in_specs=[
            pl.BlockSpec((ts, C_CONV), lambda b, j: (b * ns + j, 0)),
            pl.BlockSpec((CONV_HALO, C_CONV), lambda b, j: (jnp.maximum((b * ns + j) * hb - 1, 0), 0)),
            pl.BlockSpec((None, CONV_K, C_CONV), lambda b, j: (layer, 0, 0)),
            pl.BlockSpec((None, 1, C_CONV), lambda b, j: (layer, 0, 0)),
            pl.BlockSpec((None, 2, C_CONV), lambda b, j: (layer, 0, 0)),
        ],
        out_specs=pl.BlockSpec((ts, C_CONV), lambda b, j: (b * ns + j, 0)),
        out_shape=jax.ShapeDtypeStruct((t, C_CONV), BF16),
        scratch_shapes=[pltpu.VMEM((C_CONV // LANES, ts + CONV_HALO, LANES), F32),
                        pltpu.VMEM((ts, C_CONV), F32)],
        compiler_params=pltpu.CompilerParams(
            dimension_semantics=("parallel", "parallel"), vmem_limit_bytes=_vmem_limit(est)),
        name="conv",
    )(z, z, conv_dw, conv_b, conv_ln)


def _diff_kernel(lam_ref, q_ref, k_ref, v_ref, o_ref, vt_scr, qq_scr, m_scr, acc_scr,
                 *, tq, nq, lam_init):
    nh = q_ref.shape[1] // DV_DIFF
    row = lax.broadcasted_iota(jnp.int32, (DIFF_VT_ROWS - DV_DIFF, tq), 0)
    ones_row = jnp.where(row == 0, 1.0, 0.0).astype(BF16)
    lane = lax.broadcasted_iota(jnp.int32, (1, DV_DIFF), 1)
    comp0 = lane < HD_DIFF
    for h in range(nh):
        cols = slice(h * DV_DIFF, (h + 1) * DV_DIFF)
        for kb in range(nq):
            vt_scr[h * nq + kb, 0:DV_DIFF, :] = v_ref[kb * tq:(kb + 1) * tq, cols].T
            vt_scr[h * nq + kb, DV_DIFF:, :] = ones_row
        for qb in range(nq):
            q = q_ref[qb * tq:(qb + 1) * tq, cols]
            zero = jnp.zeros_like(q)
            qq_scr[h * nq + qb, 0:tq, :] = jnp.where(comp0, q, zero)
            qq_scr[h * nq + qb, tq:2 * tq, :] = jnp.where(comp0, zero, q)

    lamp = lam_ref[...]
    lam = (jnp.exp(jnp.sum(lamp[0:1, :] * lamp[1:2, :], keepdims=True))
           - jnp.exp(jnp.sum(lamp[2:3, :] * lamp[3:4, :], keepdims=True)) + lam_init)

    m_scr[...] = jnp.full(m_scr.shape, -jnp.inf, F32)
    acc_scr[...] = jnp.zeros(acc_scr.shape, F32)

    kc = lax.broadcasted_iota(jnp.int32, (tq, 2 * tq), 0) // CHUNK
    qc = lax.broadcasted_iota(jnp.int32, (tq, 2 * tq), 1) % tq // CHUNK
    visible = kc <= qc

    def scores(h, qb, kb):
        kblk = k_ref[kb * tq:(kb + 1) * tq, h * DV_DIFF:(h + 1) * DV_DIFF]
        st = _dot_nt(kblk, qq_scr[h * nq + qb])
        return jnp.where(visible, st, NEG_INF) if kb == qb else st

    def update(h, qb, kb, st):
        i = h * nq + qb
        m_old = m_scr[i]
        m_new = jnp.maximum(m_old, jnp.max(st, axis=0, keepdims=True))
        p = jnp.exp2(st - m_new)
        alpha = jnp.exp2(m_old - m_new)
        acc_scr[i] = alpha * acc_scr[i] + _dot(vt_scr[h * nq + kb], p.astype(BF16))
        m_scr[i] = m_new

    def finish(h, qb):
        i = h * nq + qb
        on = acc_scr[i, 0:DV_DIFF, :] * (1.0 / acc_scr[i, DV_DIFF:DV_DIFF + 1, :])
        ot = on[:, :tq] - lam * on[:, tq:]
        ot = ot * lax.rsqrt(jnp.mean(ot * ot, axis=0, keepdims=True) + EPS) * (1.0 - lam_init)
        o_ref[qb * tq:(qb + 1) * tq, h * DV_DIFF:(h + 1) * DV_DIFF] = ot.T.astype(BF16)

    visits = []
    for a in range(nq // 2):
        b = nq - 1 - a
        seq_a = [(a, a)] + [(a, kb) for kb in range(a)]
        seq_b = [(b, b)] + [(b, kb) for kb in range(b)]
        for n in range(max(len(seq_a), len(seq_b))):
            visits += seq_a[n:n + 1] + seq_b[n:n + 1]
    steps = [(h, qb, kb) for qb, kb in visits for h in range(nh)]
    last = {(h, qb): max(n for n, s in enumerate(steps) if s[:2] == (h, qb))
            for h in range(nh) for qb in range(nq)}
    pending = [scores(*s) for s in steps[:DIFF_LOOKAHEAD]]
    for n, (h, qb, kb) in enumerate(steps):
        if n + DIFF_LOOKAHEAD < len(steps):
            pending.append(scores(*steps[n + DIFF_LOOKAHEAD]))
        update(h, qb, kb, pending.pop(0))
        if last[(h, qb)] == n:
            finish(h, qb)


def _diff_attn(dq, dk, dv, diff_lambda, layer, batch, seq):
    t = dq.shape[0]
    tq = min(DIFF_TQ, seq)
    nq = seq // tq
    assert nq % 2 == 0
    lam_init = 0.8 - 0.6 * math.exp(-0.3 * layer)
    nh = DIFF_HEADS
    est = (nh * (2 * 4 * seq * DV_DIFF * 2 + nq * DIFF_VT_ROWS * tq * 2 + nq * 2 * tq * DV_DIFF * 2
                 + nq * (DIFF_VT_ROWS + 8) * 2 * tq * 4) + (DIFF_LOOKAHEAD + 4) * tq * 2 * tq * 4)
    seq_blk = pl.BlockSpec((seq, nh * DV_DIFF), lambda b, h: (b, h))
    return pl.pallas_call(
        functools.partial(_diff_kernel, tq=tq, nq=nq, lam_init=lam_init),
        grid=(batch, H_DIFF // nh),
        in_specs=[pl.BlockSpec((None, 4, HD_DIFF), lambda b, h: (layer, 0, 0)), seq_blk, seq_blk, seq_blk],
        out_specs=seq_blk,
        out_shape=jax.ShapeDtypeStruct((t, W_DIFF), BF16),
        scratch_shapes=[pltpu.VMEM((nh * nq, DIFF_VT_ROWS, tq), BF16),
                        pltpu.VMEM((nh * nq, 2 * tq, DV_DIFF), BF16),
                        pltpu.VMEM((nh * nq, 1, 2 * tq), F32),
                        pltpu.VMEM((nh * nq, DIFF_VT_ROWS, 2 * tq), F32)],
        compiler_params=pltpu.CompilerParams(
            dimension_semantics=("parallel", "parallel"), vmem_limit_bytes=_vmem_limit(est)),
        name="diff_attn",
    )(diff_lambda, dq, dk, dv)


def _chunk_kernel(gv_ref, q_ref, k_ref, v_ref, o_ref, bmt_scr, vt_scr, qq_scr, *, tq, nq, win):
    past = win - tq

    @pl.when(pl.program_id(1) == 0)
    def _():
        rc = lax.broadcasted_iota(jnp.int32, (win, tq), 0) // CHUNK
        jc = lax.broadcasted_iota(jnp.int32, (win, tq), 1) // CHUNK
        in_band = (rc >= jc) & (rc <= jc + PAST_CHUNKS)
        for e in range(2):
            ext = jnp.broadcast_to(gv_ref[e:e + 1, :], (win, 2 * win))
            toep = pltpu.roll(ext, tq, 1, stride=1, stride_axis=0)[:, :tq]
            bmt_scr[:, e * tq:(e + 1) * tq] = jnp.where(in_band, toep * LOG2_E, NEG_INF)

    vt = v_ref[...].T
    row = lax.broadcasted_iota(jnp.int32, (CH_VT_ROWS - HD_CH, vt_scr.shape[2]), 0)
    for e in range(2):
        vt_scr[e, 0:HD_CH, :] = vt[e * HD_CH:(e + 1) * HD_CH, :]
        vt_scr[e, HD_CH:, :] = jnp.where(row == 0, 1.0, 0.0).astype(BF16)

    lane = lax.broadcasted_iota(jnp.int32, (1, 2 * HD_CH), 1)
    head0 = lane < HD_CH
    for qb in range(nq):
        q = q_ref[qb * tq:(qb + 1) * tq, :]
        zero = jnp.zeros_like(q)
        qq_scr[qb, 0:tq, :] = jnp.where(head0, q, zero)
        qq_scr[qb, tq:2 * tq, :] = jnp.where(head0, zero, q)

    def window(qb):
        ws = max(qb * tq - past, 0)
        return ws, (qb + 1) * tq - ws

    def scores(qb):
        ws, wl = window(qb)
        return _dot_nt(k_ref[ws:ws + wl, :], qq_scr[qb]) + bmt_scr[win - wl:win, :]

    def finish(qb, st):
        ws, wl = window(qb)
        p = jnp.exp2(st - jnp.max(st, axis=0, keepdims=True))
        pb = p.astype(BF16)
        outs = []
        for e in range(2):
            acc = _dot(vt_scr[e, :, ws:ws + wl], pb[:, e * tq:(e + 1) * tq])
            outs.append(acc[0:HD_CH, :] * (1.0 / acc[HD_CH:HD_CH + 1, :]))
        o_ref[qb * tq:(qb + 1) * tq, :] = jnp.concatenate(outs, axis=0).T.astype(BF16)

    pending = [scores(qb) for qb in range(min(CH_LOOKAHEAD, nq))]
    for qb in range(nq):
        if qb + CH_LOOKAHEAD < nq:
            pending.append(scores(qb + CH_LOOKAHEAD))
        finish(qb, pending.pop(0))


def _chunk_attn(cq, ck, cv, gv, layer, batch, seq):
    t = cq.shape[0]
    tq = min(CH_TQ, seq)
    nq = seq // tq
    win = CH_WIN
    assert seq >= win
    est = (win * 2 * tq * 4 + 2 * CH_VT_ROWS * seq * 2 + nq * 2 * tq * 2 * HD_CH * 2 + 2 * 4 * seq * 2 * HD_CH * 2
           + win * 2 * win * 4 * 3 + (CH_LOOKAHEAD + 4) * win * 2 * tq * 4)
    seq_blk = pl.BlockSpec((seq, 2 * HD_CH), lambda hp, b: (b, hp))
    return pl.pallas_call(
        functools.partial(_chunk_kernel, tq=tq, nq=nq, win=win),
        grid=(H_CH // 2, batch),
        in_specs=[pl.BlockSpec((None, None, 2, 2 * win), lambda hp, b: (layer, hp, 0, 0)),
                  seq_blk, seq_blk, seq_blk],
        out_specs=seq_blk,
        out_shape=jax.ShapeDtypeStruct((t, W_CH), BF16),
        scratch_shapes=[pltpu.VMEM((win, 2 * tq), F32),
                        pltpu.VMEM((2, CH_VT_ROWS, seq), BF16),
                        pltpu.VMEM((nq, 2 * tq, 2 * HD_CH), BF16)],
        compiler_params=pltpu.CompilerParams(
            dimension_semantics=("parallel", "arbitrary"), vmem_limit_bytes=_vmem_limit(est)),
        name="chunk_attn",
    )(gv, cq, ck, cv)


def _bias_table_ext(rel_bias, win):
    d = jnp.arange(2 * win) - win
    rel = jnp.clip(d, -REL_FUT_MAX, REL_PAST_MAX) + REL_FUT_MAX
    ext = rel_bias[:, :, rel].astype(F32)
    return ext.reshape(rel_bias.shape[0], H_CH // 2, 2, 2 * win)


def _mix_kernel(x_ref, g_ref, za_ref, ob_ref, oc_ref, gate_ref, wa_ref, wb_ref, wc_ref, wo_ref, o_ref):
    d = D_MODEL
    ts = x_ref.shape[0] // MIX_SUB
    tiles = [slice(s * ts, (s + 1) * ts) for s in range(MIX_SUB)]
    mixes = []
    for rows in tiles:
        mix = gate_ref[rows, 0:d].astype(F32) * _dot(za_ref[rows, :], wa_ref[...])
        mix = mix + gate_ref[rows, d:2 * d].astype(F32) * _dot(ob_ref[rows, :], wb_ref[...])
        mix = mix + gate_ref[rows, 2 * d:3 * d].astype(F32) * _dot(oc_ref[rows, :], wc_ref[...])
        mixes.append(mix.astype(BF16))
    for rows, mix in zip(tiles, mixes):
        y = _dot(mix, wo_ref[...])
        o_ref[rows, :] = x_ref[rows, :] + _rms(y, g_ref[3:4, :])


def _mix(x, norm_g, za, ob, oc, gates, wa, wb, wc, wo, layer):
    t, d = x.shape
    tm = min(MIX_TM, t)
    est = (2 * 2 * tm * d * 4 + 2 * tm * (3 * C_CONV * 2 + 3 * d * 2)
           + 2 * (3 * C_CONV * d * 2 + d * d * 2) + 4 * tm * d * 4)
    row = lambda i: (i, 0)
    wspec = lambda k: pl.BlockSpec((None, k, d), lambda i: (layer, 0, 0))
    return pl.pallas_call(
        _mix_kernel,
        grid=(t // tm,),
        in_specs=[
            pl.BlockSpec((tm, d), row),
            pl.BlockSpec((None, N_NORMS, d), lambda i: (layer, 0, 0)),
            pl.BlockSpec((tm, C_CONV), row), pl.BlockSpec((tm, W_DIFF), row), pl.BlockSpec((tm, W_CH), row),
            pl.BlockSpec((tm, N_BRANCH * d), row),
            wspec(C_CONV), wspec(W_DIFF), wspec(W_CH), wspec(d),
        ],
        out_specs=pl.BlockSpec((tm, d), row),
        out_shape=jax.ShapeDtypeStruct((t, d), F32),
        compiler_params=pltpu.CompilerParams(
            dimension_semantics=("parallel",), vmem_limit_bytes=_vmem_limit(est)),
        name="mix",
    )(x, norm_g, za, ob, oc, gates, wa, wb, wc, wo)


def _rope_tables(seq):
    pos = jnp.arange(seq, dtype=F32)
    inv_freq = ROPE_THETA ** (-jnp.arange(0, HD_DIFF, 2, dtype=F32) / HD_DIFF)
    ang = pos[:, None] * inv_freq[None, :]
    reps = W_DIFF // (HD_DIFF // 2)
    cos_t = jnp.tile(jnp.cos(ang), (1, reps))
    sign = jnp.tile(jnp.concatenate([-jnp.ones(HD_DIFF // 2, F32), jnp.ones(HD_DIFF // 2, F32)]),
                    W_DIFF // HD_DIFF)
    sin_t = jnp.tile(jnp.sin(ang), (1, reps)) * sign[None, :]
    return cos_t, sin_t


def kernel(x, norm_g, ffn1_in, ffn1_out, w_in, gate_b, conv_dw, conv_b, conv_ln, w_conv_out,
           diff_lambda, w_diff_out, rel_bias, w_chunk_out, w_o, ffn2_in, ffn2_out):
    batch, seq, d = x.shape
    depth = norm_g.shape[0]
    assert d == D_MODEL and seq % CH_TQ == 0 and seq % DIFF_TQ == 0 and seq % INPROJ_TM == 0

    ffn1_in_b, ffn2_in_b = ffn1_in.astype(BF16), ffn2_in.astype(BF16)
    ffn1_out_b, ffn2_out_b = ffn1_out.astype(BF16), ffn2_out.astype(BF16)
    w_in_b = w_in.astype(BF16)
    wa_b, wb_b, wc_b, wo_b = (w.astype(BF16) for w in (w_conv_out, w_diff_out, w_chunk_out, w_o))
    gate_b3 = gate_b.reshape(depth, 1, N_BRANCH * d)
    conv_b3 = conv_b.reshape(depth, 1, C_CONV)
    cos_t, sin_t = _rope_tables(seq)
    gv = _bias_table_ext(rel_bias, CH_WIN)

    xt = x.reshape(batch * seq, d)
    for l in range(depth):
        xt = _ffn(xt, norm_g, ffn1_in_b, ffn1_out_b, l, 0, 1)
        z, dq, dk, dv, cq, ck, cv, gates = _inproj(xt, norm_g, w_in_b, gate_b3, cos_t, sin_t, l, seq)
        za = _conv(z, conv_dw, conv_b3, conv_ln, l, batch, seq)
        ob = _diff_attn(dq, dk, dv, diff_lambda, l, batch, seq)
        oc = _chunk_attn(cq, ck, cv, gv, l, batch, seq)
        xt = _mix(xt, norm_g, za, ob, oc, gates, wa_b, wb_b, wc_b, wo_b, l)
        xt = _ffn(xt, norm_g, ffn2_in_b, ffn2_out_b, l, 4, 5)
    return xt.reshape(batch, seq, d)
```
